```python
import math
import jax, jax.numpy as jnp
from jax import lax
import numpy as np

D_MODEL = 1024
BATCH = 2
SEQ = 8192
DEPTH = 4
DEC_BATCH = 32
DEC_SEQ = 8
PAST_LEN = 8192
PAGE_SIZE = 128

N_MIXERS = 2
N_POOL_LAYERS = (DEPTH + 1) // 2
N_ATTN_LAYERS = DEPTH // 2
POOL_WINDOWS = (2, 4, 8, 16)
N_POOL_GROUPS = len(POOL_WINDOWS)
POOL_GROUP = D_MODEL // N_POOL_GROUPS
POOL_BUF = max(POOL_WINDOWS) - 1
N_HEADS = 8
HEAD_DIM = D_MODEL // N_HEADS
IDX_HEADS = 8
IDX_DIM = 64
TOPK_MAX = 256
Q_BLOCK = 128
D_FF = 4 * D_MODEL
ALPHA = (2 * DEPTH) ** 0.25
BETA = (8 * DEPTH) ** -0.25
LN_EPS = 1e-5
OFF_Q = 0
OFF_K = D_MODEL
OFF_V = 2 * D_MODEL
OFF_QI = 3 * D_MODEL
OFF_KI = OFF_QI + IDX_HEADS * IDX_DIM
OFF_W = OFF_KI + IDX_DIM
D_IN = OFF_W + IDX_HEADS

kernel_name = 'pool_dsa_hybrid_step'


def layer_norm(x, g, b):
    xf = x.astype(jnp.float32)
    mu = jnp.mean(xf, axis=-1, keepdims=True)
    var = jnp.mean(jnp.square(xf - mu), axis=-1, keepdims=True)
    y = (xf - mu) * lax.rsqrt(var + LN_EPS) * g.astype(jnp.float32) + b.astype(jnp.float32)
    return y.astype(x.dtype)


def pool_mix(x, buf, start, w_grp, scale):
    B, T, D = x.shape
    xa = jnp.concatenate([buf.astype(x.dtype), x], axis=1)
    c = jnp.cumsum(xa.astype(jnp.float32), axis=1)
    c = jnp.concatenate([jnp.zeros((B, 1, D), jnp.float32), c], axis=1)
    pos = start + jnp.arange(T)
    hi = c[:, POOL_BUF + 1:POOL_BUF + 1 + T]
    xf = x.astype(jnp.float32)
    outs = []
    for g, w in enumerate(POOL_WINDOWS):
        sl = slice(g * POOL_GROUP, (g + 1) * POOL_GROUP)
        s = hi[..., sl] - c[:, POOL_BUF + 1 - w:POOL_BUF + 1 - w + T, sl]
        cnt = jnp.minimum(w, pos + 1).astype(jnp.float32)[None, :, None]
        outs.append(s / cnt - xf[..., sl])
    p = jnp.stack(outs, axis=2).astype(x.dtype)
    y = jnp.einsum('btgc,gce->btge', p, w_grp).reshape(B, T, D) * scale
    return y, xa[:, -POOL_BUF:]


def attn_project(x, w_in, kn_g, kn_b):
    B, T, _ = x.shape
    h = jnp.einsum('btd,de->bte', x, w_in)
    q = h[..., OFF_Q:OFF_K].reshape(B, T, N_HEADS, HEAD_DIM)
    k = h[..., OFF_K:OFF_V].reshape(B, T, N_HEADS, HEAD_DIM)
    v = h[..., OFF_V:OFF_QI].reshape(B, T, N_HEADS, HEAD_DIM)
    qi = h[..., OFF_QI:OFF_KI].reshape(B, T, IDX_HEADS, IDX_DIM)
    ki = layer_norm(h[..., OFF_KI:OFF_W], kn_g, kn_b)
    wt = h[..., OFF_W:] * (IDX_HEADS ** -0.5)
    return q, k, v, qi, ki, wt


def indexer_topk(qi, wt, qpos, kidx, k_sel):
    L = kidx.shape[1]
    dots = jnp.einsum('bthd,bsd->bths', qi.astype(jnp.float32), kidx.astype(jnp.float32)) * (IDX_DIM ** -0.5)
    score = jnp.einsum('bths,bth->bts', jax.nn.relu(dots), wt.astype(jnp.float32))
    score = jnp.where(jnp.arange(L)[None, None, :] <= qpos[None, :, None], score, -jnp.inf)
    _, idx = lax.top_k(score, k_sel)
    return idx


def sparse_attend(q, k_sel, v_sel, idx, qpos):
    s = jnp.einsum('bthd,btkhd->bthk', q.astype(jnp.float32), k_sel.astype(jnp.float32)) * (HEAD_DIM ** -0.5)
    valid = (idx <= qpos[None, :, None])[:, :, None, :]
    p = jax.nn.softmax(jnp.where(valid, s, -jnp.inf), axis=-1)
    o = jnp.einsum('bthk,btkhd->bthd', p, v_sel.astype(jnp.float32))
    return o.astype(q.dtype)


def gather_rows(a, i):
    return jax.vmap(lambda ab, ib: ab[ib])(a, i)


def attn_prompt(x, w_in, kn_g, kn_b, w_o):
    B, T, D = x.shape
    q, k, v, qi, ki, wt = attn_project(x, w_in, kn_g, kn_b)
    k_sel = min(TOPK_MAX, T // 4)
    nblk = T // Q_BLOCK

    def to_blocks(a):
        return a.reshape((B, nblk, Q_BLOCK) + a.shape[2:]).swapaxes(0, 1)

    def block(args):
        qb, qib, wb, pb = args
        idx = indexer_topk(qib, wb, pb, ki, k_sel)
        return sparse_attend(qb, gather_rows(k, idx), gather_rows(v, idx), idx, pb)

    pos = jnp.arange(T).reshape(nblk, Q_BLOCK)
    o = lax.map(block, (to_blocks(q), to_blocks(qi), to_blocks(wt), pos))
    o = o.swapaxes(0, 1).reshape(B, T, D)
    return jnp.einsum('btd,de->bte', o, w_o), k, v, ki


def attn_sample(x, ck, cv, cki, page_table, w_in, kn_g, kn_b, w_o):
    Bd, T, D = x.shape
    q, k, v, qi, ki, wt = attn_project(x, w_in, kn_g, kn_b)
    ki_past = cki[page_table].reshape(Bd, PAST_LEN, IDX_DIM)
    ki_all = jnp.concatenate([ki_past.astype(ki.dtype), ki], axis=1)
    k_sel = min(TOPK_MAX, (PAST_LEN + T) // 4)
    qpos = PAST_LEN + jnp.arange(T)
    idx = indexer_topk(qi, wt, qpos, ki_all, k_sel)
    in_past = (idx < PAST_LEN)[..., None, None]
    pidx = jnp.minimum(idx, PAST_LEN - 1)
    phys = jax.vmap(lambda pt, i: pt[i])(page_table, pidx // PAGE_SIZE)
    off = pidx % PAGE_SIZE
    nidx = jnp.clip(idx - PAST_LEN, 0, T - 1)
    k_g = jnp.where(in_past, ck[phys, off].astype(k.dtype), gather_rows(k, nidx))
    v_g = jnp.where(in_past, cv[phys, off].astype(v.dtype), gather_rows(v, nidx))
    o = sparse_attend(q, k_g, v_g, idx, qpos).reshape(Bd, T, D)
    return jnp.einsum('btd,de->bte', o, w_o), k, v, ki


def sq_relu_mlp(x, w1, w2):
    h = jax.nn.relu(jnp.einsum('btd,df->btf', x, w1))
    return jnp.einsum('btf,fd->btd', h * h, w2)


def setup_inputs(seed: int = 0) -> dict:
    key = jax.random.key(seed)
    ks = jax.random.split(key, 24)
    n_pages = PAST_LEN // PAGE_SIZE
    n_used = DEC_BATCH * n_pages
    n_phys = n_used + max(1, n_used // 4)
    f32 = jnp.float32
    nrm = lambda k, s: jax.random.normal(k, s, f32)
    x_prompt = nrm(ks[0], (BATCH, SEQ, D_MODEL))
    x_sample = nrm(ks[1], (DEC_BATCH, DEC_SEQ, D_MODEL))
    state_pool = nrm(ks[2], (N_POOL_LAYERS, DEC_BATCH, POOL_BUF, D_MODEL))
    cache_k = nrm(ks[3], (N_ATTN_LAYERS, n_phys, PAGE_SIZE, N_HEADS, HEAD_DIM))
    cache_v = nrm(ks[4], (N_ATTN_LAYERS, n_phys, PAGE_SIZE, N_HEADS, HEAD_DIM))
    cache_kidx = nrm(ks[5], (N_ATTN_LAYERS, n_phys, PAGE_SIZE, IDX_DIM))
    page_table = jax.random.permutation(ks[6], n_phys)[:n_used].reshape(DEC_BATCH, n_pages).astype(jnp.int32)
    pool_w = nrm(ks[7], (N_POOL_LAYERS, N_POOL_GROUPS, POOL_GROUP, POOL_GROUP)) * (POOL_GROUP ** -0.5) * BETA
    pool_scale = 1.0 + 0.02 * nrm(ks[8], (N_POOL_LAYERS, D_MODEL))
    s_in = D_MODEL ** -0.5
    w_q = nrm(ks[9], (N_ATTN_LAYERS, D_MODEL, D_MODEL)) * s_in
    w_k = nrm(ks[10], (N_ATTN_LAYERS, D_MODEL, D_MODEL)) * s_in
    w_v = nrm(ks[11], (N_ATTN_LAYERS, D_MODEL, D_MODEL)) * s_in * BETA
    w_qi = nrm(ks[12], (N_ATTN_LAYERS, D_MODEL, IDX_HEADS * IDX_DIM)) * s_in
    w_ki = nrm(ks[13], (N_ATTN_LAYERS, D_MODEL, IDX_DIM)) * s_in
    w_w = nrm(ks[14], (N_ATTN_LAYERS, D_MODEL, IDX_HEADS)) * s_in
    attn_w_in = jnp.concatenate([w_q, w_k, w_v, w_qi, w_ki, w_w], axis=-1)
    attn_kn_g = 1.0 + 0.02 * nrm(ks[15], (N_ATTN_LAYERS, IDX_DIM))
    attn_kn_b = 0.02 * nrm(ks[16], (N_ATTN_LAYERS, IDX_DIM))
    attn_w_o = nrm(ks[17], (N_ATTN_LAYERS, D_MODEL, D_MODEL)) * s_in * BETA
    mlp_w1 = nrm(ks[18], (DEPTH, D_MODEL, D_FF)) * s_in
    mlp_w2 = nrm(ks[19], (DEPTH, D_FF, D_MODEL)) * (D_FF ** -0.5) * BETA
    ln_g = 1.0 + 0.02 * nrm(ks[20], (DEPTH, 2, D_MODEL))
    ln_b = 0.02 * nrm(ks[21], (DEPTH, 2, D_MODEL))
    return {'x_prompt': x_prompt, 'x_sample': x_sample, 'state_pool': state_pool,
            'cache_k': cache_k, 'cache_v': cache_v, 'cache_kidx': cache_kidx, 'page_table': page_table,
            'pool_w': pool_w, 'pool_scale': pool_scale, 'attn_w_in': attn_w_in,
            'attn_kn_g': attn_kn_g, 'attn_kn_b': attn_kn_b, 'attn_w_o': attn_w_o,
            'mlp_w1': mlp_w1, 'mlp_w2': mlp_w2, 'ln_g': ln_g, 'ln_b': ln_b}


def reference(x_prompt, x_sample, state_pool, cache_k, cache_v, cache_kidx, page_table,
              pool_w, pool_scale, attn_w_in, attn_kn_g, attn_kn_b, attn_w_o,
              mlp_w1, mlp_w2, ln_g, ln_b):
    xp, xs = x_prompt, x_sample
    pool_p, pool_s = [], []
    kp, vp, kip, ksm, vsm, kism = [], [], [], [], [], []
    for i in range(DEPTH):
        j = i // N_MIXERS
        if i % N_MIXERS == 0:
            zero_buf = jnp.zeros((xp.shape[0], POOL_BUF, D_MODEL), xp.dtype)
            hp, bp = pool_mix(xp, zero_buf, 0, pool_w[j], pool_scale[j])
            hs, bs = pool_mix(xs, state_pool[j], PAST_LEN, pool_w[j], pool_scale[j])
            pool_p.append(bp)
            pool_s.append(bs)
        else:
            hp, k1, v1, ki1 = attn_prompt(xp, attn_w_in[j], attn_kn_g[j], attn_kn_b[j], attn_w_o[j])
            hs, k2, v2, ki2 = attn_sample(xs, cache_k[j], cache_v[j], cache_kidx[j], page_table,
                                          attn_w_in[j], attn_kn_g[j], attn_kn_b[j], attn_w_o[j])
            kp.append(k1); vp.append(v1); kip.append(ki1)
            ksm.append(k2); vsm.append(v2); kism.append(ki2)
        xp = layer_norm(ALPHA * xp + hp, ln_g[i, 0], ln_b[i, 0])
        xs = layer_norm(ALPHA * xs + hs, ln_g[i, 0], ln_b[i, 0])
        xp = layer_norm(ALPHA * xp + sq_relu_mlp(xp, mlp_w1[i], mlp_w2[i]), ln_g[i, 1], ln_b[i, 1])
        xs = layer_norm(ALPHA * xs + sq_relu_mlp(xs, mlp_w1[i], mlp_w2[i]), ln_g[i, 1], ln_b[i, 1])
    return (xp, xs, jnp.stack(pool_p), jnp.stack(pool_s), jnp.stack(kp), jnp.stack(vp), jnp.stack(kip),
            jnp.stack(ksm), jnp.stack(vsm), jnp.stack(kism))
```

```python
import functools

import numpy as np
import jax
import jax.numpy as jnp
from jax import lax
from jax.experimental import pallas as pl
from jax.experimental.pallas import tpu as pltpu

D_MODEL = 1024
DEPTH = 4
PAST_LEN = 8192
PAGE_SIZE = 128
POOL_WINDOWS = (2, 4, 8, 16)
POOL_GROUP = D_MODEL // len(POOL_WINDOWS)
POOL_BUF = max(POOL_WINDOWS) - 1
HALO = POOL_BUF + 1
N_HEADS = 8
HEAD_DIM = D_MODEL // N_HEADS
IDX_HEADS = 8
IDX_DIM = 64
TOPK_MAX = 256
D_FF = 4 * D_MODEL
ALPHA = (2 * DEPTH) ** 0.25
LN_EPS = 1e-5
D_QI = IDX_HEADS * IDX_DIM

LANES = 128
NEG_BIAS = -1e30
KEY_NEG_INF = -2139095041
KEY_POS_INF = 2139095040
VMEM_LIMIT = 52 * 1024 * 1024

F32 = jnp.float32
BF16 = jnp.bfloat16


def _ln(y, g, b):
    mu = jnp.mean(y, axis=-1, keepdims=True)
    yc = y - mu
    var = jnp.mean(yc * yc, axis=-1, keepdims=True)
    return yc * lax.rsqrt(var + LN_EPS) * g + b


def _dot_nt(a, b):
    return lax.dot_general(a, b, (((1,), (1,)), ((), ())), preferred_element_type=F32)


def _pool_kernel(x_ref, halo_ref, w_ref, sc_ref, g_ref, b_ref, o_ref, *, tq, start, first_is_zero):
    i = pl.program_id(1)
    x = x_ref[0]
    halo = halo_ref[0]
    if first_is_zero:
        halo = jnp.where(i == 0, 0.0, halo)
    xa = jnp.concatenate([halo, x], axis=0)
    pos = start + i * tq + lax.broadcasted_iota(jnp.int32, (tq, 1), 0)
    outs = []
    for g, w in enumerate(POOL_WINDOWS):
        sl = slice(g * POOL_GROUP, (g + 1) * POOL_GROUP)
        s = xa[:, sl]
        sh = 1
        while sh < w:
            s = s + pltpu.roll(s, sh, axis=0)
            sh *= 2
        cnt = jnp.minimum(w, pos + 1).astype(F32)
        p = s[HALO:] / cnt - x[:, sl]
        outs.append(jnp.dot(p.astype(BF16), w_ref[g], preferred_element_type=F32))
    y = jnp.concatenate(outs, axis=1) * sc_ref[...]
    o_ref[0] = _ln(ALPHA * x + y, g_ref[...], b_ref[...])


def _pool_layer(x, halo, w_bf, scale, g, b, *, tq, start, first_is_zero):
    B, T, D = x.shape
    nq = T // tq
    if first_is_zero:
        halo_spec = pl.BlockSpec((1, HALO, D), lambda bb, i: (bb, jnp.maximum(i * (tq // HALO) - 1, 0), 0))
    else:
        halo_spec = pl.BlockSpec((1, HALO, D), lambda bb, i: (bb, 0, 0))
    row = lambda: pl.BlockSpec((1, D), lambda bb, i: (0, 0))
    return pl.pallas_call(
        functools.partial(_pool_kernel, tq=tq, start=start, first_is_zero=first_is_zero),
        grid=(B, nq),
        in_specs=[pl.BlockSpec((1, tq, D), lambda bb, i: (bb, i, 0)),
                  halo_spec,
                  pl.BlockSpec((len(POOL_WINDOWS), POOL_GROUP, POOL_GROUP), lambda bb, i: (0, 0, 0)),
                  row(), row(), row()],
        out_specs=pl.BlockSpec((1, tq, D), lambda bb, i: (bb, i, 0)),
        out_shape=jax.ShapeDtypeStruct((B, T, D), F32),
        compiler_params=pltpu.CompilerParams(dimension_semantics=("arbitrary", "arbitrary")),
        name="pool_mix_ln",
    )(x, halo, w_bf, scale, g, b)


def _mlp_kernel(x_ref, w1_ref, w2_ref, g_ref, b_ref, o_ref, *, ffc):
    x = x_ref[...]
    xb = x.astype(BF16)
    acc = jnp.zeros(x.shape, F32)
    for c in range(D_FF // ffc):
        h = jnp.dot(xb, w1_ref[:, c * ffc:(c + 1) * ffc], preferred_element_type=F32)
        h = jnp.maximum(h, 0.0)
        acc = acc + jnp.dot((h * h).astype(BF16), w2_ref[c * ffc:(c + 1) * ffc, :],
                            preferred_element_type=F32)
    o_ref[...] = _ln(ALPHA * x + acc, g_ref[...], b_ref[...])


def _mlp_layer(x2, w1_bf, w2_bf, g, b, *, tm):
    n, D = x2.shape
    row = lambda: pl.BlockSpec((1, D), lambda i: (0, 0))
    return pl.pallas_call(
        functools.partial(_mlp_kernel, ffc=1024),
        grid=(n // tm,),
        in_specs=[pl.BlockSpec((tm, D), lambda i: (i, 0)),
                  pl.BlockSpec((D, D_FF), lambda i: (0, 0), pipeline_mode=pl.Buffered(1)),
                  pl.BlockSpec((D_FF, D), lambda i: (0, 0), pipeline_mode=pl.Buffered(1)),
                  row(), row()],
        out_specs=pl.BlockSpec((tm, D), lambda i: (i, 0)),
        out_shape=jax.ShapeDtypeStruct((n, D), F32),
        compiler_params=pltpu.CompilerParams(dimension_semantics=("arbitrary",),
                                             vmem_limit_bytes=VMEM_LIMIT),
        name="mlp_ln",
    )(x2, w1_bf, w2_bf, g, b)


def _proj_kernel(x_ref, wqkv_ref, wqi_ref, wkw_ref, kng_ref, knb_ref,
                 q_ref, k_ref, v_ref, kb_ref, vb_ref, qi_ref, ki_ref, wt_ref):
    xb = x_ref[...].astype(BF16)
    for part, (f_ref, b_ref) in enumerate(((None, q_ref), (k_ref, kb_ref), (v_ref, vb_ref))):
        h = jnp.dot(xb, wqkv_ref[:, part * D_MODEL:(part + 1) * D_MODEL], preferred_element_type=F32)
        if f_ref is not None:
            f_ref[...] = h
        b_ref[...] = h.astype(BF16)
    hqi = jnp.dot(xb, wqi_ref[...], preferred_element_type=F32)
    for hh in range(IDX_HEADS):
        qi_ref[hh] = hqi[:, hh * IDX_DIM:(hh + 1) * IDX_DIM].astype(BF16)
    hkw = jnp.dot(xb, wkw_ref[...], preferred_element_type=F32)
    ki_ref[...] = _ln(hkw[:, :IDX_DIM], kng_ref[...], knb_ref[...])
    wt_ref[...] = hkw[:, IDX_DIM:IDX_DIM + IDX_HEADS] * (IDX_HEADS ** -0.5)


def _proj_layer(x2, wqkv_bf, wqi_bf, wkw_bf, kn_g, kn_b, *, tm):
    n, D = x2.shape
    full = lambda shp: pl.BlockSpec(shp, lambda i: tuple(0 for _ in shp))
    rows = lambda c: pl.BlockSpec((tm, c), lambda i: (i, 0))
    sds = jax.ShapeDtypeStruct
    return pl.pallas_call(
        _proj_kernel,
        grid=(n // tm,),
        in_specs=[rows(D), full((D, 3 * D_MODEL)), full((D, D_QI)), full((D, LANES)),
                  full((1, IDX_DIM)), full((1, IDX_DIM))],
        out_specs=[rows(D), rows(D), rows(D), rows(D), rows(D),
                   pl.BlockSpec((IDX_HEADS, tm, IDX_DIM), lambda i: (0, i, 0)),
                   rows(IDX_DIM), rows(IDX_HEADS)],
        out_shape=[sds((n, D), BF16), sds((n, D), F32), sds((n, D), F32), sds((n, D), BF16),
                   sds((n, D), BF16), sds((IDX_HEADS, n, IDX_DIM), BF16),
                   sds((n, IDX_DIM), F32), sds((n, IDX_HEADS), F32)],
        compiler_params=pltpu.CompilerParams(dimension_semantics=("arbitrary",),
                                             vmem_limit_bytes=VMEM_LIMIT),
        name="attn_in_proj",
    )(x2, wqkv_bf, wqi_bf, wkw_bf, kn_g, kn_b)


def _oproj_kernel(o_ref, x_ref, wo_ref, g_ref, b_ref, y_ref):
    h = jnp.dot(o_ref[...], wo_ref[...], preferred_element_type=F32)
    y_ref[...] = _ln(ALPHA * x_ref[...] + h, g_ref[...], b_ref[...])


def _oproj_layer(o2, x2, wo_bf, g, b, *, tm):
    n, D = x2.shape
    row = lambda: pl.BlockSpec((1, D), lambda i: (0, 0))
    return pl.pallas_call(
        _oproj_kernel,
        grid=(n // tm,),
        in_specs=[pl.BlockSpec((tm, D), lambda i: (i, 0)), pl.BlockSpec((tm, D), lambda i: (i, 0)),
                  pl.BlockSpec((D, D), lambda i: (0, 0)), row(), row()],
        out_specs=pl.BlockSpec((tm, D), lambda i: (i, 0)),
        out_shape=jax.ShapeDtypeStruct((n, D), F32),
        compiler_params=pltpu.CompilerParams(dimension_semantics=("arbitrary",)),
        name="attn_out_proj_ln",
    )(o2, x2, wo_bf, g, b)


def _key_to_f32(key):
    bits = key ^ ((key >> 31) & jnp.int32(0x7FFFFFFF))
    return lax.bitcast_convert_type(bits, F32)


def _chunk_loop(n_chunks, body, init):
    if isinstance(n_chunks, int):
        st = init
        for c in range(n_chunks):
            st = body(c, st)
        return st
    return lax.fori_loop(0, n_chunks, body, init)


def _chunk_start(c, cw):
    return c * cw if isinstance(c, int) else pl.multiple_of(c * cw, cw)


def _select_bias(sc_ref, r0, nr, n_chunks, cw, tpos, ksel):
    rows = slice(r0, r0 + nr)
    ncols = n_chunks * cw
    nt = cw // LANES
    lane = lax.broadcasted_iota(jnp.int32, (nr, LANES), 1)

    def load(c):
        return sc_ref[rows, pl.ds(_chunk_start(c, cw), cw)]

    def count(pred_tile, c0=None):
        def body(c, acc):
            x = load(c)
            for j in range(nt):
                acc = acc + pred_tile(x[:, j * LANES:(j + 1) * LANES], c * cw + j * LANES)
            return acc
        acc = _chunk_loop(n_chunks, body, jnp.zeros((nr, LANES), F32))
        return jnp.sum(acc, axis=1, keepdims=True)

    def bis(_, st):
        lo, hi, clo, chi = st
        mid = (lo & hi) + ((lo ^ hi) >> 1)
        thr = jnp.broadcast_to(_key_to_f32(mid), (nr, LANES))
        cnt = count(lambda x, base: jnp.where(x >= thr, 1.0, 0.0))
        ge = cnt >= ksel
        return (jnp.where(ge, mid, lo), jnp.where(ge, hi, mid),
                jnp.where(ge, cnt, clo), jnp.where(ge, chi, cnt))

    ncols_f = jnp.asarray(ncols, F32)
    init = (jnp.full((nr, 1), KEY_NEG_INF, jnp.int32), jnp.full((nr, 1), KEY_POS_INF + 1, jnp.int32),
            jnp.zeros((nr, 1), F32) + ncols_f, jnp.zeros((nr, 1), F32))
    lo, _, clo, chi = lax.fori_loop(0, 32, bis, init)
    thr = jnp.broadcast_to(_key_to_f32(lo), (nr, LANES))

    tie_rows = clo > ksel
    need = ksel - chi
    ncols_i = jnp.asarray(ncols, jnp.int32)

    def tie_phase():
        def tb(_, st):
            jlo, jhi = st
            mid = (jlo + jhi) >> 1
            midb = jnp.broadcast_to(mid, (nr, LANES))
            cnt = count(lambda x, base: jnp.where(x == thr, jnp.where(lane + base <= midb, 1.0, 0.0), 0.0))
            ok = cnt >= need
            return jnp.where(ok, jlo, mid), jnp.where(ok, mid, jhi)
        init_j = (jnp.full((nr, 1), -1, jnp.int32), jnp.zeros((nr, 1), jnp.int32) + (ncols_i - 1))
        _, jhi = lax.fori_loop(0, 14, tb, init_j)
        return jnp.where(tie_rows, jhi, ncols_i)

    any_tie = jnp.max(jnp.where(tie_rows, 1.0, 0.0)) > 0.0
    jcut = lax.cond(any_tie, tie_phase, lambda: jnp.zeros((nr, 1), jnp.int32) + ncols_i)
    jb = jnp.broadcast_to(jnp.minimum(jcut, tpos), (nr, LANES))

    def wb(c, carry):
        x = load(c)
        outs = []
        for j in range(nt):
            xt = x[:, j * LANES:(j + 1) * LANES]
            idx = lane + (c * cw + j * LANES)
            tie_sel = jnp.where(idx <= jb, 0.0, NEG_BIAS)
            outs.append(jnp.where(xt > thr, 0.0, jnp.where(xt == thr, tie_sel, NEG_BIAS)))
        sc_ref[rows, pl.ds(_chunk_start(c, cw), cw)] = jnp.concatenate(outs, axis=1) if nt > 1 else outs[0]
        return carry
    _chunk_loop(n_chunks, wb, 0)


def _flash_update(h, s, v_h, m_ref, l_ref, acc_ref):
    hs = slice(h * HEAD_DIM, (h + 1) * HEAD_DIM)
    m_old = m_ref[h]
    m_new = jnp.maximum(m_old, jnp.max(s, axis=1, keepdims=True))
    a = jnp.exp(m_old - m_new)
    p = jnp.exp(s - m_new)
    l_ref[h] = a * l_ref[h] + jnp.sum(p, axis=1, keepdims=True)
    acc_ref[:, hs] = a * acc_ref[:, hs] + jnp.dot(p.astype(BF16), v_h, preferred_element_type=F32)
    m_ref[h] = m_new


def _flash_init(m_ref, l_ref, acc_ref):
    m_ref[...] = jnp.full(m_ref.shape, NEG_BIAS, F32)
    l_ref[...] = jnp.zeros(l_ref.shape, F32)
    acc_ref[...] = jnp.zeros(acc_ref.shape, F32)


def _flash_finish(o_ref_store, l_ref, acc_ref):
    outs = []
    for h in range(N_HEADS):
        hs = slice(h * HEAD_DIM, (h + 1) * HEAD_DIM)
        outs.append(acc_ref[:, hs] / l_ref[h])
    o_ref_store(jnp.concatenate(outs, axis=1).astype(BF16))


def _attn_prompt_kernel(qmap_ref, kmap_ref, q_ref, qi_ref, wt_ref, ki_ref, k_ref, v_ref, o_ref,
                        sc_ref, m_ref, l_ref, acc_ref, *, tq, tk, rsel, ksel):
    n = pl.program_id(1)
    i = qmap_ref[n]
    kj = kmap_ref[n]
    last = ((i + 1) * tq - 1) // tk

    @pl.when(kj == 0)
    def _():
        wts = wt_ref[0] * (IDX_DIM ** -0.5)
        wb = [jnp.broadcast_to(wts[:, h:h + 1], (tq, LANES)) for h in range(IDX_HEADS)]
        tpos = i * tq + lax.broadcasted_iota(jnp.int32, (tq, 1), 0)

        def chunk_scores(c):
            kic = ki_ref[0, pl.ds(pl.multiple_of(c * tk, tk), tk), :].astype(BF16)
            sc = jnp.zeros((tq, tk), F32)
            for h in range(IDX_HEADS):
                d = jnp.maximum(_dot_nt(qi_ref[h, 0], kic), 0.0)
                sc = sc + d * jnp.concatenate([wb[h]] * (tk // LANES), axis=1)
            return sc

        def body(c, carry):
            sc_ref[:, pl.ds(pl.multiple_of(c * tk, tk), tk)] = chunk_scores(c)
            return carry
        lax.fori_loop(0, last, body, 0)
        kidx = last * tk + lax.broadcasted_iota(jnp.int32, (tq, tk), 1)
        sc_ref[:, pl.ds(pl.multiple_of(last * tk, tk), tk)] = jnp.where(kidx <= tpos, chunk_scores(last), -jnp.inf)

        for r in range(tq // rsel):
            _select_bias(sc_ref, r * rsel, rsel, last + 1, tk, tpos[r * rsel:(r + 1) * rsel], ksel)
        _flash_init(m_ref, l_ref, acc_ref)

    bias = sc_ref[:, pl.ds(pl.multiple_of(kj * tk, tk), tk)]
    for h in range(N_HEADS):
        hs = slice(h * HEAD_DIM, (h + 1) * HEAD_DIM)
        s = _dot_nt(q_ref[0, :, hs], k_ref[0, :, hs]) * (HEAD_DIM ** -0.5) + bias
        _flash_update(h, s, v_ref[0, :, hs], m_ref, l_ref, acc_ref)

    @pl.when(kj == last)
    def _():
        def store(o):
            o_ref[0] = o
        _flash_finish(store, l_ref, acc_ref)


def _attn_prompt(q, qi, wt, ki, kb, vb, *, tq=256, tk=512):
    B, T, D = q.shape
    ksel = min(TOPK_MAX, T // 4)
    nq = T // tq
    qmap, kmap = [], []
    for i in range(nq):
        for kj in range(((i + 1) * tq - 1) // tk + 1):
            qmap.append(i)
            kmap.append(kj)
    qmap = jnp.asarray(np.asarray(qmap, np.int32))
    kmap = jnp.asarray(np.asarray(kmap, np.int32))
    grid_spec = pltpu.PrefetchScalarGridSpec(
        num_scalar_prefetch=2,
        grid=(B, int(qmap.shape[0])),
        in_specs=[pl.BlockSpec((1, tq, D), lambda b, n, qm, km: (b, qm[n], 0)),
                  pl.BlockSpec((IDX_HEADS, 1, tq, IDX_DIM), lambda b, n, qm, km: (0, b, qm[n], 0)),
                  pl.BlockSpec((1, tq, IDX_HEADS), lambda b, n, qm, km: (b, qm[n], 0)),
                  pl.BlockSpec((1, T, IDX_DIM), lambda b, n, qm, km: (b, 0, 0)),
                  pl.BlockSpec((1, tk, D), lambda b, n, qm, km: (b, km[n], 0)),
                  pl.BlockSpec((1, tk, D), lambda b, n, qm, km: (b, km[n], 0))],
        out_specs=pl.BlockSpec((1, tq, D), lambda b, n, qm, km: (b, qm[n], 0)),
        scratch_shapes=[pltpu.VMEM((tq, T), F32),
                        pltpu.VMEM((N_HEADS, tq, 1), F32),
                        pltpu.VMEM((N_HEADS, tq, 1), F32),
                        pltpu.VMEM((tq, D), F32)])
    return pl.pallas_call(
        functools.partial(_attn_prompt_kernel, tq=tq, tk=tk, rsel=64, ksel=ksel),
        grid_spec=grid_spec,
        out_shape=jax.ShapeDtypeStruct((B, T, D), BF16),
        compiler_params=pltpu.CompilerParams(dimension_semantics=("arbitrary", "arbitrary"),
                                             vmem_limit_bytes=VMEM_LIMIT),
        name="attn_prompt",
    )(qmap, kmap, q, qi, wt, ki, kb, vb)


KI_PAGES_PER_STEP = 8
KV_PAGES_PER_STEP = 4


def _attn_sample_kernel(pt_ref, q_ref, qi_ref, wt_ref, kin_ref, kn_ref, vn_ref, *rest,
                        layer, n_pages, tq, ksel):
    del pt_ref, layer
    g1, g2 = KI_PAGES_PER_STEP, KV_PAGES_PER_STEP
    kidx_refs = rest[:g1]
    kpage_refs = rest[g1:g1 + g2]
    vpage_refs = rest[g1 + g2:g1 + 2 * g2]
    o_ref, sc_ref, m_ref, l_ref, acc_ref = rest[g1 + 2 * g2:]
    ns1 = n_pages // g1
    ns2 = n_pages // g2
    s = pl.program_id(1)
    n_chunks = n_pages + 1

    def chunk_scores(ki_chunk):
        d = jnp.maximum(_dot_nt(qi_ref[0], ki_chunk.astype(BF16)), 0.0)
        d = d * jnp.broadcast_to(wt_ref[0] * (IDX_DIM ** -0.5), d.shape)
        sc = d[0:tq]
        for h in range(1, IDX_HEADS):
            sc = sc + d[h * tq:(h + 1) * tq]
        return sc

    @pl.when(s < ns1)
    def _():
        for g in range(g1):
            page = s * g1 + g
            sc_ref[:, pl.ds(pl.multiple_of(page * PAGE_SIZE, PAGE_SIZE), PAGE_SIZE)] = chunk_scores(kidx_refs[g][0, 0])

    @pl.when(s == ns1 - 1)
    def _():
        qrow = lax.broadcasted_iota(jnp.int32, (tq, PAGE_SIZE), 0)
        jcol = lax.broadcasted_iota(jnp.int32, (tq, PAGE_SIZE), 1)
        sc_ref[:, n_pages * PAGE_SIZE:] = jnp.where(jcol <= qrow, chunk_scores(kin_ref[0]), -jnp.inf)
        tpos = n_pages * PAGE_SIZE + lax.broadcasted_iota(jnp.int32, (tq, 1), 0)
        _select_bias(sc_ref, 0, tq, n_chunks, PAGE_SIZE, tpos, ksel)
        _flash_init(m_ref, l_ref, acc_ref)

    def attend(bias, k_of_head, v_of_head):
        for h in range(N_HEADS):
            hs = slice(h * HEAD_DIM, (h + 1) * HEAD_DIM)
            sco = _dot_nt(q_ref[0, :, hs], k_of_head(h)) * (HEAD_DIM ** -0.5) + bias
            _flash_update(h, sco, v_of_head(h), m_ref, l_ref, acc_ref)

    @pl.when(s >= ns1)
    def _():
        for g in range(g2):
            page = (s - ns1) * g2 + g
            bias = sc_ref[:, pl.ds(pl.multiple_of(page * PAGE_SIZE, PAGE_SIZE), PAGE_SIZE)]
            attend(bias,
                   lambda h: kpage_refs[g][0, 0, pl.ds(h, PAGE_SIZE, stride=N_HEADS), :].astype(BF16),
                   lambda h: vpage_refs[g][0, 0, pl.ds(h, PAGE_SIZE, stride=N_HEADS), :].astype(BF16))

    @pl.when(s == ns1 + ns2 - 1)
    def _():
        bias = sc_ref[:, n_pages * PAGE_SIZE:]
        attend(bias,
               lambda h: kn_ref[0, :, h * HEAD_DIM:(h + 1) * HEAD_DIM],
               lambda h: vn_ref[0, :, h * HEAD_DIM:(h + 1) * HEAD_DIM])

        def store(o):
            o_ref[0] = o
        _flash_finish(store, l_ref, acc_ref)


def _attn_sample(layer, page_table, q, qi, wt, ki_new, kb_new, vb_new, cache_kidx, cache_k, cache_v):
    Bd, tq, D = q.shape
    n_pages = page_table.shape[1]
    ksel = min(TOPK_MAX, (n_pages * PAGE_SIZE + tq) // 4)
    g1, g2 = KI_PAGES_PER_STEP, KV_PAGES_PER_STEP
    ns1, ns2 = n_pages // g1, n_pages // g2

    def per_b(shape):
        return pl.BlockSpec((1,) + shape, lambda b, s, pt: (b,) + tuple(0 for _ in shape))

    def kidx_spec(g):
        return pl.BlockSpec((1, 1, PAGE_SIZE, IDX_DIM),
                            lambda b, s, pt: (layer, pt[b, jnp.minimum(s, ns1 - 1) * g1 + g], 0, 0))

    def kv_spec(g):
        return pl.BlockSpec((1, 1, PAGE_SIZE * N_HEADS, HEAD_DIM),
                            lambda b, s, pt: (layer, pt[b, jnp.maximum(s - ns1, 0) * g2 + g], 0, 0))

    grid_spec = pltpu.PrefetchScalarGridSpec(
        num_scalar_prefetch=1,
        grid=(Bd, ns1 + ns2),
        in_specs=[per_b((tq, D)), per_b((IDX_HEADS * tq, IDX_DIM)), per_b((IDX_HEADS * tq, 1)),
                  per_b((PAGE_SIZE, IDX_DIM)), per_b((PAGE_SIZE, D)), per_b((PAGE_SIZE, D))]
                 + [kidx_spec(g) for g in range(g1)]
                 + [kv_spec(g) for g in range(g2)] + [kv_spec(g) for g in range(g2)],
        out_specs=per_b((tq, D)),
        scratch_shapes=[pltpu.VMEM((tq, (n_pages + 1) * PAGE_SIZE), F32),
                        pltpu.VMEM((N_HEADS, tq, 1), F32),
                        pltpu.VMEM((N_HEADS, tq, 1), F32),
                        pltpu.VMEM((tq, D), F32)])
    return pl.pallas_call(
        functools.partial(_attn_sample_kernel, layer=layer, n_pages=n_pages, tq=tq, ksel=ksel),
        grid_spec=grid_spec,
        out_shape=jax.ShapeDtypeStruct((Bd, tq, D), BF16),
        compiler_params=pltpu.CompilerParams(dimension_semantics=("arbitrary", "arbitrary"),
                                             vmem_limit_bytes=VMEM_LIMIT),
        name="attn_sample",
    )(page_table, q, qi, wt, ki_new, kb_new, vb_new,
      *([cache_kidx] * g1), *([cache_k] * g2), *([cache_v] * g2))


def _pad_rows(a, n):
    return jnp.pad(a, ((0, 0), (0, n - a.shape[1]), (0, 0)))


def kernel(x_prompt, x_sample, state_pool, cache_k, cache_v, cache_kidx, page_table, pool_w, pool_scale,
           attn_w_in, attn_kn_g, attn_kn_b, attn_w_o, mlp_w1, mlp_w2, ln_g, ln_b):
    B, T, D = x_prompt.shape
    Bd, Td, _ = x_sample.shape
    n_attn, n_phys = cache_k.shape[0], cache_k.shape[1]
    xp, xs = x_prompt, x_sample
    ck = cache_k.reshape(n_attn, n_phys, PAGE_SIZE * N_HEADS, HEAD_DIM)
    cv = cache_v.reshape(n_attn, n_phys, PAGE_SIZE * N_HEADS, HEAD_DIM)
    pool_p, pool_s = [], []
    kp, vp, kip, ksm, vsm, kism = [], [], [], [], [], []
    for i in range(DEPTH):
        j = i // 2
        g0, b0 = ln_g[i, 0][None], ln_b[i, 0][None]
        g1, b1 = ln_g[i, 1][None], ln_b[i, 1][None]
        if i % 2 == 0:
            w_bf = pool_w[j].astype(BF16)
            scale = pool_scale[j][None]
            pool_p.append(xp[:, T - POOL_BUF:])
            halo_s = jnp.concatenate([jnp.zeros((Bd, 1, D), xs.dtype), state_pool[j].astype(xs.dtype)], axis=1)
            pool_s.append(jnp.concatenate([halo_s, xs], axis=1)[:, -POOL_BUF:])
            xp = _pool_layer(xp, xp, w_bf, scale, g0, b0, tq=512, start=0, first_is_zero=True)
            xs = _pool_layer(xs, halo_s, w_bf, scale, g0, b0, tq=Td, start=PAST_LEN, first_is_zero=False)
        else:
            w_in = attn_w_in[j]
            wqkv = w_in[:, :3 * D_MODEL].astype(BF16)
            wqi = w_in[:, 3 * D_MODEL:3 * D_MODEL + D_QI].astype(BF16)
            wkw = jnp.pad(w_in[:, 3 * D_MODEL + D_QI:], ((0, 0), (0, LANES - IDX_DIM - IDX_HEADS))).astype(BF16)
            kng, knb = attn_kn_g[j][None], attn_kn_b[j][None]
            wo = attn_w_o[j].astype(BF16)

            q, k, v, kb, vb, qi, ki, wt = _proj_layer(xp.reshape(B * T, D), wqkv, wqi, wkw, kng, knb, tm=512)
            o = _attn_prompt(q.reshape(B, T, D), qi.reshape(IDX_HEADS, B, T, IDX_DIM), wt.reshape(B, T, IDX_HEADS),
                             ki.reshape(B, T, IDX_DIM), kb.reshape(B, T, D), vb.reshape(B, T, D))
            xp = _oproj_layer(o.reshape(B * T, D), xp.reshape(B * T, D), wo, g0, b0, tm=512).reshape(B, T, D)
            kp.append(k.reshape(B, T, N_HEADS, HEAD_DIM))
            vp.append(v.reshape(B, T, N_HEADS, HEAD_DIM))
            kip.append(ki.reshape(B, T, IDX_DIM))

            n = Bd * Td
            q, k, v, kb, vb, qi, ki, wt = _proj_layer(xs.reshape(n, D), wqkv, wqi, wkw, kng, knb, tm=n)
            qi_s = qi.reshape(IDX_HEADS, Bd, Td, IDX_DIM).transpose(1, 0, 2, 3).reshape(Bd, IDX_HEADS * Td, IDX_DIM)
            wt_s = wt.reshape(Bd, Td, IDX_HEADS).transpose(0, 2, 1).reshape(Bd, IDX_HEADS * Td, 1)
            o = _attn_sample(j, page_table, q.reshape(Bd, Td, D), qi_s, wt_s,
                             _pad_rows(ki.reshape(Bd, Td, IDX_DIM), PAGE_SIZE),
                             _pad_rows(kb.reshape(Bd, Td, D), PAGE_SIZE),
                             _pad_rows(vb.reshape(Bd, Td, D), PAGE_SIZE),
                             cache_kidx, ck, cv)
            xs = _oproj_layer(o.reshape(n, D), xs.reshape(n, D), wo, g0, b0, tm=n).reshape(Bd, Td, D)
            ksm.append(k.reshape(Bd, Td, N_HEADS, HEAD_DIM))
            vsm.append(v.reshape(Bd, Td, N_HEADS, HEAD_DIM))
            kism.append(ki.reshape(Bd, Td, IDX_DIM))
        w1, w2 = mlp_w1[i].astype(BF16), mlp_w2[i].astype(BF16)
        xp = _mlp_layer(xp.reshape(B * T, D), w1, w2, g1, b1, tm=512).reshape(B, T, D)
        xs = _mlp_layer(xs.reshape(Bd * Td, D), w1, w2, g1, b1, tm=Bd * Td).reshape(Bd, Td, D)
    return (xp, xs, jnp.stack(pool_p), jnp.stack(pool_s), jnp.stack(kp), jnp.stack(vp), jnp.stack(kip),
            jnp.stack(ksm), jnp.stack(vsm), jnp.stack(kism))
```

```python
import functools
import math

import numpy as np
import jax
import jax.numpy as jnp
from jax import lax
from jax.experimental import pallas as pl
from jax.experimental.pallas import tpu as pltpu

D_MODEL = 1024
DEPTH = 4
PAST_LEN = 8192
PAGE_SIZE = 128
POOL_WINDOWS = (2, 4, 8, 16)
POOL_GROUP = D_MODEL // len(POOL_WINDOWS)
POOL_BUF = max(POOL_WINDOWS) - 1
HALO = POOL_BUF + 1
N_HEADS = 8
HEAD_DIM = D_MODEL // N_HEADS
IDX_HEADS = 8
IDX_DIM = 64
TOPK_MAX = 256
D_FF = 4 * D_MODEL
ALPHA = (2 * DEPTH) ** 0.25
LN_EPS = 1e-5
D_QI = IDX_HEADS * IDX_DIM
Q_SCALE = HEAD_DIM ** -0.5 * math.log2(math.e)

LANES = 128
NEG_BIAS = -1e30
F32_MAX = float(np.finfo(np.float32).max)
F32_TINY = float(np.finfo(np.float32).tiny)
VMEM_LIMIT = 52 * 1024 * 1024
SEARCH_MAX_STEPS = 400
SEL_SUB = 128

F32 = jnp.float32
BF16 = jnp.bfloat16


def _ln(y, g, b):
    mu = jnp.mean(y, axis=-1, keepdims=True)
    yc = y - mu
    var = jnp.mean(yc * yc, axis=-1, keepdims=True)
    return yc * lax.rsqrt(var + LN_EPS) * g + b


def _dot_nt(a, b):
    return lax.dot_general(a, b, (((1,), (1,)), ((), ())), preferred_element_type=F32)


def _pool_kernel(x_ref, halo_ref, w_ref, sc_ref, g_ref, b_ref, o_ref, *, tq, start, first_is_zero):
    i = pl.program_id(1)
    x = x_ref[0]
    halo = halo_ref[0]
    if first_is_zero:
        halo = jnp.where(i == 0, 0.0, halo)
    xa = jnp.concatenate([halo, x], axis=0)
    pos = start + i * tq + lax.broadcasted_iota(jnp.int32, (tq, 1), 0)
    outs = []
    for g, w in enumerate(POOL_WINDOWS):
        sl = slice(g * POOL_GROUP, (g + 1) * POOL_GROUP)
        s = xa[:, sl]
        sh = 1
        while sh < w:
            s = s + pltpu.roll(s, sh, axis=0)
            sh *= 2
        cnt = jnp.minimum(w, pos + 1).astype(F32)
        p = s[HALO:] / cnt - x[:, sl]
        outs.append(jnp.dot(p.astype(BF16), w_ref[g], preferred_element_type=F32))
    y = jnp.concatenate(outs, axis=1) * sc_ref[...]
    o_ref[0] = _ln(ALPHA * x + y, g_ref[...], b_ref[...])


def _pool_layer(x, halo, w_bf, scale, g, b, *, tq, start, first_is_zero):
    B, T, D = x.shape
    nq = T // tq
    if first_is_zero:
        halo_spec = pl.BlockSpec((1, HALO, D), lambda bb, i: (bb, jnp.maximum(i * (tq // HALO) - 1, 0), 0))
    else:
        halo_spec = pl.BlockSpec((1, HALO, D), lambda bb, i: (bb, 0, 0))
    row = lambda: pl.BlockSpec((1, D), lambda bb, i: (0, 0))
    return pl.pallas_call(
        functools.partial(_pool_kernel, tq=tq, start=start, first_is_zero=first_is_zero),
        grid=(B, nq),
        in_specs=[pl.BlockSpec((1, tq, D), lambda bb, i: (bb, i, 0)),
                  halo_spec,
                  pl.BlockSpec((len(POOL_WINDOWS), POOL_GROUP, POOL_GROUP), lambda bb, i: (0, 0, 0)),
                  row(), row(), row()],
        out_specs=pl.BlockSpec((1, tq, D), lambda bb, i: (bb, i, 0)),
        out_shape=jax.ShapeDtypeStruct((B, T, D), F32),
        compiler_params=pltpu.CompilerParams(dimension_semantics=("arbitrary", "arbitrary")),
        name="pool_mix_ln",
    )(x, halo, w_bf, scale, g, b)


def _mlp_kernel(x_ref, w1_ref, w2_ref, g_ref, b_ref, o_ref, *, ffc):
    x = x_ref[...]
    xb = x.astype(BF16)
    acc = jnp.zeros(x.shape, F32)
    for c in range(D_FF // ffc):
        h = jnp.dot(xb, w1_ref[:, c * ffc:(c + 1) * ffc], preferred_element_type=F32)
        h = jnp.maximum(h, 0.0)
        acc = acc + jnp.dot((h * h).astype(BF16), w2_ref[c * ffc:(c + 1) * ffc, :],
                            preferred_element_type=F32)
    o_ref[...] = _ln(ALPHA * x + acc, g_ref[...], b_ref[...])


def _mlp_layer(x2, w1_bf, w2_bf, g, b, *, tm):
    n, D = x2.shape
    row = lambda: pl.BlockSpec((1, D), lambda i: (0, 0))
    return pl.pallas_call(
        functools.partial(_mlp_kernel, ffc=1024),
        grid=(n // tm,),
        in_specs=[pl.BlockSpec((tm, D), lambda i: (i, 0)),
                  pl.BlockSpec((D, D_FF), lambda i: (0, 0), pipeline_mode=pl.Buffered(1)),
                  pl.BlockSpec((D_FF, D), lambda i: (0, 0), pipeline_mode=pl.Buffered(1)),
                  row(), row()],
        out_specs=pl.BlockSpec((tm, D), lambda i: (i, 0)),
        out_shape=jax.ShapeDtypeStruct((n, D), F32),
        compiler_params=pltpu.CompilerParams(dimension_semantics=("arbitrary",),
                                             vmem_limit_bytes=VMEM_LIMIT),
        name="mlp_ln",
    )(x2, w1_bf, w2_bf, g, b)


def _proj_kernel(x_ref, wqkv_ref, wqi_ref, wkw_ref, kng_ref, knb_ref,
                 q_ref, k_ref, v_ref, kb_ref, vb_ref, qi_ref, ki_ref, wt_ref):
    xb = x_ref[...].astype(BF16)
    hq = jnp.dot(xb, wqkv_ref[:, :D_MODEL], preferred_element_type=F32)
    q_ref[...] = (hq * Q_SCALE).astype(BF16)
    for part, (f_ref, b_ref) in enumerate(((k_ref, kb_ref), (v_ref, vb_ref)), start=1):
        h = jnp.dot(xb, wqkv_ref[:, part * D_MODEL:(part + 1) * D_MODEL], preferred_element_type=F32)
        f_ref[...] = h
        b_ref[...] = h.astype(BF16)
    hqi = jnp.dot(xb, wqi_ref[...], preferred_element_type=F32)
    for hh in range(IDX_HEADS):
        qi_ref[hh] = hqi[:, hh * IDX_DIM:(hh + 1) * IDX_DIM].astype(BF16)
    hkw = jnp.dot(xb, wkw_ref[...], preferred_element_type=F32)
    ki_ref[...] = _ln(hkw[:, :IDX_DIM], kng_ref[...], knb_ref[...])
    wt_ref[...] = hkw[:, IDX_DIM:IDX_DIM + IDX_HEADS] * (IDX_HEADS ** -0.5)


def _proj_layer(x2, wqkv_bf, wqi_bf, wkw_bf, kn_g, kn_b, *, tm):
    n, D = x2.shape
    full = lambda shp: pl.BlockSpec(shp, lambda i: tuple(0 for _ in shp))
    rows = lambda c: pl.BlockSpec((tm, c), lambda i: (i, 0))
    sds = jax.ShapeDtypeStruct
    return pl.pallas_call(
        _proj_kernel,
        grid=(n // tm,),
        in_specs=[rows(D), full((D, 3 * D_MODEL)), full((D, D_QI)), full((D, LANES)),
                  full((1, IDX_DIM)), full((1, IDX_DIM))],
        out_specs=[rows(D), rows(D), rows(D), rows(D), rows(D),
                   pl.BlockSpec((IDX_HEADS, tm, IDX_DIM), lambda i: (0, i, 0)),
                   rows(IDX_DIM), rows(IDX_HEADS)],
        out_shape=[sds((n, D), BF16), sds((n, D), F32), sds((n, D), F32), sds((n, D), BF16),
                   sds((n, D), BF16), sds((IDX_HEADS, n, IDX_DIM), BF16),
                   sds((n, IDX_DIM), F32), sds((n, IDX_HEADS), F32)],
        compiler_params=pltpu.CompilerParams(dimension_semantics=("arbitrary",),
                                             vmem_limit_bytes=VMEM_LIMIT),
        name="attn_in_proj",
    )(x2, wqkv_bf, wqi_bf, wkw_bf, kn_g, kn_b)


def _oproj_kernel(o_ref, x_ref, wo_ref, g_ref, b_ref, y_ref):
    h = jnp.dot(o_ref[...], wo_ref[...], preferred_element_type=F32)
    y_ref[...] = _ln(ALPHA * x_ref[...] + h, g_ref[...], b_ref[...])


def _oproj_layer(o2, x2, wo_bf, g, b, *, tm):
    n, D = x2.shape
    row = lambda: pl.BlockSpec((1, D), lambda i: (0, 0))
    return pl.pallas_call(
        _oproj_kernel,
        grid=(n // tm,),
        in_specs=[pl.BlockSpec((tm, D), lambda i: (i, 0)), pl.BlockSpec((tm, D), lambda i: (i, 0)),
                  pl.BlockSpec((D, D), lambda i: (0, 0)), row(), row()],
        out_specs=pl.BlockSpec((tm, D), lambda i: (i, 0)),
        out_shape=jax.ShapeDtypeStruct((n, D), F32),
        compiler_params=pltpu.CompilerParams(dimension_semantics=("arbitrary",)),
        name="attn_out_proj_ln",
    )(o2, x2, wo_bf, g, b)


def _select_bias(sc_ref, r0, nr, n_chunks, cw, tpos, ksel):
    sub = min(nr, SEL_SUB)
    nsub = nr // sub
    ncols = n_chunks * cw
    nt = cw // LANES
    lane = lax.broadcasted_iota(jnp.int32, (sub, LANES), 1)
    inf = jnp.inf

    def fold(tile_fn, init, cols=(), store=False):
        res = []
        for r in range(nsub):
            rows = slice(r0 + r * sub, r0 + (r + 1) * sub)
            bc = [jnp.broadcast_to(c[:, r * sub:(r + 1) * sub], (LANES, sub)).T for c in cols]

            def body(c, accs, rows=rows, bc=bc):
                cs = pl.ds(pl.multiple_of(c * cw, cw), cw)
                x = sc_ref[rows, cs]
                outs = []
                for j in range(nt):
                    accs = tile_fn(accs, x[:, j * LANES:(j + 1) * LANES], c * cw + j * LANES, bc)
                    if store:
                        outs.append(accs)
                if store:
                    sc_ref[rows, cs] = jnp.concatenate(outs, axis=1) if nt > 1 else outs[0]
                    return 0
                return accs
            res.append(lax.fori_loop(0, n_chunks, body, init))
        return res

    def row_reduce(parts, k, op):
        return jnp.concatenate([op(p[k].T, axis=0, keepdims=True) for p in parts], axis=1)

    def count(pred, cols):
        parts = fold(lambda a, x, base, bc: (a[0] + pred(x, base, bc),), (jnp.zeros((sub, LANES), F32),), cols)
        return row_reduce(parts, 0, jnp.sum)

    parts = fold(lambda a, x, base, bc: (jnp.maximum(a[0], x), jnp.minimum(a[1], jnp.where(x == -inf, inf, x))),
                 (jnp.full((sub, LANES), -inf, F32), jnp.full((sub, LANES), inf, F32)))
    rmax = row_reduce(parts, 0, jnp.max)
    rmin = row_reduce(parts, 1, jnp.min)

    def status(flo, fhi, clo):
        mid = 0.5 * flo + 0.5 * fhi
        lo_inf = flo == -inf
        hi_inf = fhi == inf
        p = mid
        p = jnp.where((flo == 0.0) & (fhi > F32_TINY), F32_TINY, p)
        p = jnp.where((flo < 0.0) & (fhi > 0.0), 0.0, p)
        p = jnp.where(lo_inf, jnp.where(fhi > rmin, rmin, -F32_MAX), p)
        p = jnp.where(hi_inf, rmax, p)
        adjacent = ~lo_inf & ~hi_inf & ((mid <= flo) | (mid >= fhi))
        done = (clo == ksel) | adjacent | (hi_inf & (flo >= rmax)) | (lo_inf & (fhi <= -F32_MAX))
        return jnp.where(done, 1.0, 0.0), p

    def search_body(st):
        it, flo, fhi, clo, chi, donef, p, _ = st
        cnt = count(lambda x, base, bc: jnp.where(x >= bc[0], 1.0, 0.0), (p,))
        live = donef < 0.5
        up_lo = live & (cnt >= ksel)
        up_hi = live & (cnt < ksel)
        flo, clo = jnp.where(up_lo, p, flo), jnp.where(up_lo, cnt, clo)
        fhi, chi = jnp.where(up_hi, p, fhi), jnp.where(up_hi, cnt, chi)
        donef, p = status(flo, fhi, clo)
        return it + 1, flo, fhi, clo, chi, donef, p, (jnp.min(donef) > 0.5).astype(jnp.int32)

    ncols_f = jnp.asarray(ncols).astype(F32)
    flo0 = jnp.full((1, nr), -inf, F32)
    fhi0 = jnp.full((1, nr), inf, F32)
    clo0 = jnp.zeros((1, nr), F32) + ncols_f
    done0, p0 = status(flo0, fhi0, clo0)
    st = lax.while_loop(lambda st: (st[0] < SEARCH_MAX_STEPS) & (st[7] == 0), search_body,
                        (jnp.int32(0), flo0, fhi0, clo0, jnp.zeros((1, nr), F32), done0, p0, jnp.int32(0)))
    _, thr, _, clo, chi, _, _, _ = st

    tie_rows = clo > ksel
    need = ksel - chi
    ncols_i = jnp.asarray(ncols).astype(jnp.int32)

    def tie_phase():
        def tb(_, st):
            jlo, jhi = st
            mid = (jlo + jhi) >> 1
            cnt = count(lambda x, base, bc: jnp.where(x == bc[0], jnp.where(lane + base <= bc[1], 1.0, 0.0), 0.0),
                        (thr, mid))
            ok = cnt >= need
            return jnp.where(ok, jlo, mid), jnp.where(ok, mid, jhi)
        init_j = (jnp.full((1, nr), -1, jnp.int32), jnp.zeros((1, nr), jnp.int32) + (ncols_i - 1))
        _, jhi = lax.fori_loop(0, 14, tb, init_j)
        return jnp.where(tie_rows, jhi, ncols_i)

    any_tie = jnp.max(jnp.where(tie_rows, 1.0, 0.0)) > 0.0
    jcut = lax.cond(any_tie, tie_phase, lambda: jnp.zeros((1, nr), jnp.int32) + ncols_i)

    def bias_tile(_, x, base, bc):
        tie_sel = jnp.where(lane + base <= bc[1], 0.0, NEG_BIAS)
        return jnp.where(x > bc[0], 0.0, jnp.where(x == bc[0], tie_sel, NEG_BIAS))
    fold(bias_tile, 0, (thr, jnp.minimum(jcut, tpos)), store=True)


SEL_ROWS = 256


def _attn_prompt_kernel(qmap_ref, kmap_ref, q_ref, qi_ref, wt_ref, ki_ref, k_ref, v_ref, o_ref,
                        sc_ref, wb_ref, m_ref, l_ref, a_ref, acc_ref, s_ref, p_ref, *, tq, tk, ksel):
    n = pl.program_id(1)
    i = qmap_ref[n]
    kj = kmap_ref[n]
    last = ((i + 1) * tq - 1) // tk

    @pl.when(kj == 0)
    def _():
        wts = wt_ref[0] * (IDX_DIM ** -0.5)
        for h in range(IDX_HEADS):
            wb_ref[h] = jnp.broadcast_to(wts[:, h:h + 1], (tq, LANES))
        tpos = i * tq + lax.broadcasted_iota(jnp.int32, (tq, 1), 0)

        def chunk_scores(c, masked):
            cols = pl.ds(pl.multiple_of(c * tk, tk), tk)
            kic = ki_ref[0, cols, :].astype(BF16)
            sc = None
            for h in range(IDX_HEADS):
                d = jnp.maximum(_dot_nt(qi_ref[h, 0], kic), 0.0)
                d = d * jnp.concatenate([wb_ref[h]] * (tk // LANES), axis=1)
                sc = d if sc is None else sc + d
            if masked:
                kidx = c * tk + lax.broadcasted_iota(jnp.int32, (tq, tk), 1)
                sc = jnp.where(kidx <= tpos, sc, -jnp.inf)
            sc_ref[:, cols] = sc

        def body(c, carry):
            chunk_scores(c, False)
            return carry
        lax.fori_loop(0, last, body, 0)
        chunk_scores(last, True)

        for r in range(tq // SEL_ROWS):
            tpos_row = i * tq + r * SEL_ROWS + lax.broadcasted_iota(jnp.int32, (1, SEL_ROWS), 1)
            _select_bias(sc_ref, r * SEL_ROWS, SEL_ROWS, last + 1, tk, tpos_row, ksel)
        m_ref[...] = jnp.full(m_ref.shape, NEG_BIAS, F32)
        l_ref[...] = jnp.zeros(l_ref.shape, F32)
        acc_ref[...] = jnp.zeros(acc_ref.shape, F32)

    cols = pl.ds(pl.multiple_of(kj * tk, tk), tk)
    heads = [slice(h * HEAD_DIM, (h + 1) * HEAD_DIM) for h in range(N_HEADS)]
    rep = tk // LANES
    for h, hs in enumerate(heads):
        s = _dot_nt(q_ref[0, :, hs], k_ref[0, :, hs]) + sc_ref[:, cols]
        s_ref[h] = s
        m_old = m_ref[h]
        m_new = jnp.maximum(m_old, jnp.broadcast_to(jnp.max(s, axis=1, keepdims=True), (tq, LANES)))
        a_ref[h] = jnp.exp2(m_old - m_new)
        m_ref[h] = m_new
    for h, hs in enumerate(heads):
        p = jnp.exp2(s_ref[h] - jnp.concatenate([m_ref[h]] * rep, axis=1))
        l_ref[h] = a_ref[h][:, :1] * l_ref[h] + jnp.sum(p, axis=1, keepdims=True)
        p_ref[h] = p.astype(BF16)
    for h, hs in enumerate(heads):
        acc_ref[:, hs] = a_ref[h] * acc_ref[:, hs] + jnp.dot(p_ref[h], v_ref[0, :, hs],
                                                             preferred_element_type=F32)

    @pl.when(kj == last)
    def _():
        for h in range(N_HEADS):
            hs = slice(h * HEAD_DIM, (h + 1) * HEAD_DIM)
            o_ref[0, :, hs] = (acc_ref[:, hs] / l_ref[h]).astype(BF16)


def _attn_prompt(q, qi, wt, ki, kb, vb, *, tq=256, tk=512):
    B, T, D = q.shape
    ksel = min(TOPK_MAX, T // 4)
    nq = T // tq
    qmap, kmap = [], []
    for i in range(nq):
        for kj in range(((i + 1) * tq - 1) // tk + 1):
            qmap.append(i)
            kmap.append(kj)
    qmap = jnp.asarray(np.asarray(qmap, np.int32))
    kmap = jnp.asarray(np.asarray(kmap, np.int32))
    grid_spec = pltpu.PrefetchScalarGridSpec(
        num_scalar_prefetch=2,
        grid=(B, int(qmap.shape[0])),
        in_specs=[pl.BlockSpec((1, tq, D), lambda b, n, qm, km: (b, qm[n], 0)),
                  pl.BlockSpec((IDX_HEADS, 1, tq, IDX_DIM), lambda b, n, qm, km: (0, b, qm[n], 0)),
                  pl.BlockSpec((1, tq, IDX_HEADS), lambda b, n, qm, km: (b, qm[n], 0)),
                  pl.BlockSpec((1, T, IDX_DIM), lambda b, n, qm, km: (b, 0, 0)),
                  pl.BlockSpec((1, tk, D), lambda b, n, qm, km: (b, km[n], 0)),
                  pl.BlockSpec((1, tk, D), lambda b, n, qm, km: (b, km[n], 0))],
        out_specs=pl.BlockSpec((1, tq, D), lambda b, n, qm, km: (b, qm[n], 0)),
        scratch_shapes=[pltpu.VMEM((tq, T), F32),
                        pltpu.VMEM((IDX_HEADS, tq, LANES), F32),
                        pltpu.VMEM((N_HEADS, tq, LANES), F32),
                        pltpu.VMEM((N_HEADS, tq, 1), F32),
                        pltpu.VMEM((N_HEADS, tq, LANES), F32),
                        pltpu.VMEM((tq, D), F32),
                        pltpu.VMEM((N_HEADS, tq, tk), F32),
                        pltpu.VMEM((N_HEADS, tq, tk), BF16)])
    return pl.pallas_call(
        functools.partial(_attn_prompt_kernel, tq=tq, tk=tk, ksel=ksel),
        grid_spec=grid_spec,
        out_shape=jax.ShapeDtypeStruct((B, T, D), BF16),
        compiler_params=pltpu.CompilerParams(dimension_semantics=("arbitrary", "arbitrary"),
                                             vmem_limit_bytes=VMEM_LIMIT),
        name="attn_prompt",
    )(qmap, kmap, q, qi, wt, ki, kb, vb)


KI_PAGES_PER_STEP = 16
KV_PAGES_PER_STEP = 8
SEL_CHUNK = 5 * LANES


def _sample_scores_kernel(pt_ref, qi_ref, wt_ref, kin_ref, *rest, n_pages, tq):
    del pt_ref
    g1 = KI_PAGES_PER_STEP
    kidx_refs, sc_ref = rest[:g1], rest[g1]
    s = pl.program_id(1)
    wcol = jnp.broadcast_to(wt_ref[0] * (IDX_DIM ** -0.5), (IDX_HEADS * tq, PAGE_SIZE))

    def chunk_scores(ki_chunk):
        d = jnp.maximum(_dot_nt(qi_ref[0], ki_chunk.astype(BF16)), 0.0) * wcol
        sc = d[0:tq]
        for h in range(1, IDX_HEADS):
            sc = sc + d[h * tq:(h + 1) * tq]
        return sc

    for g in range(g1):
        page = s * g1 + g
        sc_ref[0, :, pl.ds(pl.multiple_of(page * PAGE_SIZE, PAGE_SIZE), PAGE_SIZE)] = chunk_scores(kidx_refs[g][0, 0])

    @pl.when(s == n_pages // g1 - 1)
    def _():
        qrow = lax.broadcasted_iota(jnp.int32, (tq, PAGE_SIZE), 0)
        jcol = lax.broadcasted_iota(jnp.int32, (tq, PAGE_SIZE), 1)
        sc_ref[0, :, n_pages * PAGE_SIZE:] = jnp.where(jcol <= qrow, chunk_scores(kin_ref[0]), -jnp.inf)


def _sample_scores(layer, page_table, qi, wt, ki_new, cache_kidx, *, tq):
    Bd, n_pages = page_table.shape
    g1 = KI_PAGES_PER_STEP
    ncols = (n_pages + 1) * PAGE_SIZE

    def per_b(shape):
        return pl.BlockSpec((1,) + shape, lambda b, s, pt: (b,) + tuple(0 for _ in shape))

    def kidx_spec(g):
        return pl.BlockSpec((1, 1, PAGE_SIZE, IDX_DIM), lambda b, s, pt: (layer, pt[b, s * g1 + g], 0, 0))

    grid_spec = pltpu.PrefetchScalarGridSpec(
        num_scalar_prefetch=1,
        grid=(Bd, n_pages // g1),
        in_specs=[per_b((IDX_HEADS * tq, IDX_DIM)), per_b((IDX_HEADS * tq, 1)), per_b((PAGE_SIZE, IDX_DIM))]
                 + [kidx_spec(g) for g in range(g1)],
        out_specs=per_b((tq, ncols)))
    return pl.pallas_call(
        functools.partial(_sample_scores_kernel, n_pages=n_pages, tq=tq),
        grid_spec=grid_spec,
        out_shape=jax.ShapeDtypeStruct((Bd, tq, ncols), F32),
        compiler_params=pltpu.CompilerParams(dimension_semantics=("arbitrary", "arbitrary")),
        name="sample_scores",
    )(page_table, qi, wt, ki_new, *([cache_kidx] * g1))


def _sample_select_kernel(sc_ref, o_ref, *, tq, past, ksel):
    o_ref[...] = sc_ref[...]
    nr, ncols = o_ref.shape
    tpos = past + lax.rem(lax.broadcasted_iota(jnp.int32, (1, nr), 1), tq)
    _select_bias(o_ref, 0, nr, ncols // SEL_CHUNK, SEL_CHUNK, tpos, ksel)


def _sample_select(sc2, *, tq, past, ksel):
    n, ncols = sc2.shape
    nr = min(SEL_ROWS, n)
    return pl.pallas_call(
        functools.partial(_sample_select_kernel, tq=tq, past=past, ksel=ksel),
        grid=(n // nr,),
        in_specs=[pl.BlockSpec((nr, ncols), lambda i: (i, 0))],
        out_specs=pl.BlockSpec((nr, ncols), lambda i: (i, 0)),
        out_shape=jax.ShapeDtypeStruct((n, ncols), F32),
        compiler_params=pltpu.CompilerParams(dimension_semantics=("arbitrary",), vmem_limit_bytes=VMEM_LIMIT),
        name="sample_select",
    )(sc2)


def _sample_attend_kernel(pt_ref, q_ref, bias_ref, kn_ref, vn_ref, *rest, n_pages, tq):
    del pt_ref
    g2 = KV_PAGES_PER_STEP
    kpage_refs, vpage_refs = rest[:g2], rest[g2:2 * g2]
    o_ref, s_ref, l_ref, acc_ref = rest[2 * g2:]
    nsk = n_pages // g2
    s = pl.program_id(1)
    new_cols = slice(n_pages * PAGE_SIZE, (n_pages + 1) * PAGE_SIZE)

    def head_rows(h):
        return slice(h * tq, (h + 1) * tq)

    def head_cols(h):
        return slice(h * HEAD_DIM, (h + 1) * HEAD_DIM)

    @pl.when(s < nsk)
    def _():
        for g in range(g2):
            cols = pl.ds(pl.multiple_of((s * g2 + g) * PAGE_SIZE, PAGE_SIZE), PAGE_SIZE)
            bias = bias_ref[0, :, cols]
            for h in range(N_HEADS):
                kh = kpage_refs[g][0, 0, pl.ds(h, PAGE_SIZE, stride=N_HEADS), :].astype(BF16)
                s_ref[head_rows(h), cols] = _dot_nt(q_ref[0, :, head_cols(h)], kh) + bias

    @pl.when(s == nsk - 1)
    def _():
        bias = bias_ref[0, :, new_cols]
        for h in range(N_HEADS):
            s_ref[head_rows(h), new_cols] = _dot_nt(q_ref[0, :, head_cols(h)], kn_ref[0, :, head_cols(h)]) + bias
        sall = s_ref[...]
        p = jnp.exp2(sall - jnp.max(sall, axis=1, keepdims=True))
        l_ref[...] = jnp.sum(p, axis=1, keepdims=True)
        s_ref[...] = p
        acc_ref[...] = jnp.zeros(acc_ref.shape, F32)

    @pl.when(s >= nsk)
    def _():
        for h in range(N_HEADS):
            o = jnp.zeros((tq, HEAD_DIM), F32)
            for g in range(g2):
                cols = pl.ds(pl.multiple_of(((s - nsk) * g2 + g) * PAGE_SIZE, PAGE_SIZE), PAGE_SIZE)
                vh = vpage_refs[g][0, 0, pl.ds(h, PAGE_SIZE, stride=N_HEADS), :].astype(BF16)
                o = o + jnp.dot(s_ref[head_rows(h), cols].astype(BF16), vh, preferred_element_type=F32)
            acc_ref[:, head_cols(h)] += o

    @pl.when(s == 2 * nsk - 1)
    def _():
        for h in range(N_HEADS):
            o = acc_ref[:, head_cols(h)] + jnp.dot(s_ref[head_rows(h), new_cols].astype(BF16),
                                                   vn_ref[0, :, head_cols(h)], preferred_element_type=F32)
            o_ref[0, :, head_cols(h)] = (o / l_ref[head_rows(h), :]).astype(BF16)


def _sample_attend(layer, page_table, q, bias, kb_new, vb_new, cache_k, cache_v):
    Bd, tq, D = q.shape
    n_pages = page_table.shape[1]
    ncols = (n_pages + 1) * PAGE_SIZE
    g2 = KV_PAGES_PER_STEP
    nsk = n_pages // g2

    def per_b(shape):
        return pl.BlockSpec((1,) + shape, lambda b, s, pt: (b,) + tuple(0 for _ in shape))

    def page_spec(g, first_step):
        return pl.BlockSpec((1, 1, PAGE_SIZE * N_HEADS, HEAD_DIM),
                            lambda b, s, pt: (layer, pt[b, jnp.clip(s - first_step, 0, nsk - 1) * g2 + g], 0, 0))

    grid_spec = pltpu.PrefetchScalarGridSpec(
        num_scalar_prefetch=1,
        grid=(Bd, 2 * nsk),
        in_specs=[per_b((tq, D)), per_b((tq, ncols)), per_b((PAGE_SIZE, D)), per_b((PAGE_SIZE, D))]
                 + [page_spec(g, 0) for g in range(g2)] + [page_spec(g, nsk) for g in range(g2)],
        out_specs=per_b((tq, D)),
        scratch_shapes=[pltpu.VMEM((N_HEADS * tq, ncols), F32),
                        pltpu.VMEM((N_HEADS * tq, 1), F32),
                        pltpu.VMEM((tq, D), F32)])
    return pl.pallas_call(
        functools.partial(_sample_attend_kernel, n_pages=n_pages, tq=tq),
        grid_spec=grid_spec,
        out_shape=jax.ShapeDtypeStruct((Bd, tq, D), BF16),
        compiler_params=pltpu.CompilerParams(dimension_semantics=("arbitrary", "arbitrary"),
                                             vmem_limit_bytes=VMEM_LIMIT),
        name="sample_attend",
    )(page_table, q, bias, kb_new, vb_new, *([cache_k] * g2), *([cache_v] * g2))


def _pad_rows(a, n):
    return jnp.pad(a, ((0, 0), (0, n - a.shape[1]), (0, 0)))


def kernel(x_prompt, x_sample, state_pool, cache_k, cache_v, cache_kidx, page_table, pool_w, pool_scale,
           attn_w_in, attn_kn_g, attn_kn_b, attn_w_o, mlp_w1, mlp_w2, ln_g, ln_b):
    B, T, D = x_prompt.shape
    Bd, Td, _ = x_sample.shape
    n_attn, n_phys = cache_k.shape[0], cache_k.shape[1]
    n_pages = page_table.shape[1]
    past = n_pages * PAGE_SIZE
    xp, xs = x_prompt, x_sample
    ck = cache_k.reshape(n_attn, n_phys, PAGE_SIZE * N_HEADS, HEAD_DIM)
    cv = cache_v.reshape(n_attn, n_phys, PAGE_SIZE * N_HEADS, HEAD_DIM)
    pool_p, pool_s = [], []
    kp, vp, kip, ksm, vsm, kism = [], [], [], [], [], []
    for i in range(DEPTH):
        j = i // 2
        g0, b0 = ln_g[i, 0][None], ln_b[i, 0][None]
        g1, b1 = ln_g[i, 1][None], ln_b[i, 1][None]
        if i % 2 == 0:
            w_bf = pool_w[j].astype(BF16)
            scale = pool_scale[j][None]
            pool_p.append(xp[:, T - POOL_BUF:])
            halo_s = jnp.concatenate([jnp.zeros((Bd, 1, D), xs.dtype), state_pool[j].astype(xs.dtype)], axis=1)
            pool_s.append(jnp.concatenate([halo_s, xs], axis=1)[:, -POOL_BUF:])
            xp = _pool_layer(xp, xp, w_bf, scale, g0, b0, tq=512, start=0, first_is_zero=True)
            xs = _pool_layer(xs, halo_s, w_bf, scale, g0, b0, tq=Td, start=past, first_is_zero=False)
        else:
            w_in = attn_w_in[j]
            wqkv = w_in[:, :3 * D_MODEL].astype(BF16)
            wqi = w_in[:, 3 * D_MODEL:3 * D_MODEL + D_QI].astype(BF16)
            wkw = jnp.pad(w_in[:, 3 * D_MODEL + D_QI:], ((0, 0), (0, LANES - IDX_DIM - IDX_HEADS))).astype(BF16)
            kng, knb = attn_kn_g[j][None], attn_kn_b[j][None]
            wo = attn_w_o[j].astype(BF16)

            q, k, v, kb, vb, qi, ki, wt = _proj_layer(xp.reshape(B * T, D), wqkv, wqi, wkw, kng, knb, tm=512)
            o = _attn_prompt(q.reshape(B, T, D), qi.reshape(IDX_HEADS, B, T, IDX_DIM), wt.reshape(B, T, IDX_HEADS),
                             ki.reshape(B, T, IDX_DIM), kb.reshape(B, T, D), vb.reshape(B, T, D))
            xp = _oproj_layer(o.reshape(B * T, D), xp.reshape(B * T, D), wo, g0, b0, tm=512).reshape(B, T, D)
            kp.append(k.reshape(B, T, N_HEADS, HEAD_DIM))
            vp.append(v.reshape(B, T, N_HEADS, HEAD_DIM))
            kip.append(ki.reshape(B, T, IDX_DIM))

            n = Bd * Td
            q, k, v, kb, vb, qi, ki, wt = _proj_layer(xs.reshape(n, D), wqkv, wqi, wkw, kng, knb, tm=n)
            qi_s = qi.reshape(IDX_HEADS, Bd, Td, IDX_DIM).transpose(1, 0, 2, 3).reshape(Bd, IDX_HEADS * Td, IDX_DIM)
            wt_s = wt.reshape(Bd, Td, IDX_HEADS).transpose(0, 2, 1).reshape(Bd, IDX_HEADS * Td, 1)
            sc = _sample_scores(j, page_table, qi_s, wt_s, _pad_rows(ki.reshape(Bd, Td, IDX_DIM), PAGE_SIZE),
                                cache_kidx, tq=Td)
            bias = _sample_select(sc.reshape(n, sc.shape[-1]), tq=Td, past=past,
                                  ksel=min(TOPK_MAX, (past + Td) // 4))
            o = _sample_attend(j, page_table, q.reshape(Bd, Td, D), bias.reshape(Bd, Td, -1),
                               _pad_rows(kb.reshape(Bd, Td, D), PAGE_SIZE),
                               _pad_rows(vb.reshape(Bd, Td, D), PAGE_SIZE), ck, cv)
            xs = _oproj_layer(o.reshape(n, D), xs.reshape(n, D), wo, g0, b0, tm=n).reshape(Bd, Td, D)
            ksm.append(k.reshape(Bd, Td, N_HEADS, HEAD_DIM))
            vsm.append(v.reshape(Bd, Td, N_HEADS, HEAD_DIM))
            kism.append(ki.reshape(Bd, Td, IDX_DIM))
        w1, w2 = mlp_w1[i].astype(BF16), mlp_w2[i].astype(BF16)
        xp = _mlp_layer(xp.reshape(B * T, D), w1, w2, g1, b1, tm=512).reshape(B, T, D)
        xs = _mlp_layer(xs.reshape(Bd * Td, D), w1, w2, g1, b1, tm=Bd * Td).reshape(Bd, Td, D)
    return (xp, xs, jnp.stack(pool_p), jnp.stack(pool_s), jnp.stack(kp), jnp.stack(vp), jnp.stack(kip),
            jnp.stack(ksm), jnp.stack(vsm), jnp.stack(kism))
```

```python
import functools
import math

import numpy as np
import jax
import jax.numpy as jnp
from jax import lax
from jax.experimental import pallas as pl
from jax.experimental.pallas import tpu as pltpu

D_MODEL = 1024
DEPTH = 4
PAST_LEN = 8192
PAGE_SIZE = 128
POOL_WINDOWS = (2, 4, 8, 16)
POOL_GROUP = D_MODEL // len(POOL_WINDOWS)
POOL_BUF = max(POOL_WINDOWS) - 1
HALO = POOL_BUF + 1
N_HEADS = 8
HEAD_DIM = D_MODEL // N_HEADS
IDX_HEADS = 8
IDX_DIM = 64
TOPK_MAX = 256
D_FF = 4 * D_MODEL
ALPHA = (2 * DEPTH) ** 0.25
LN_EPS = 1e-5
D_QI = IDX_HEADS * IDX_DIM
Q_SCALE = HEAD_DIM ** -0.5 * math.log2(math.e)

LANES = 128
NEG_BIAS = -1e30
F32_MAX = float(np.finfo(np.float32).max)
F32_TINY = float(np.finfo(np.float32).tiny)
VMEM_LIMIT = 52 * 1024 * 1024
SEARCH_MAX_STEPS = 400
SEL_SUB = 128

F32 = jnp.float32
BF16 = jnp.bfloat16


def _ln(y, g, b):
    mu = jnp.mean(y, axis=-1, keepdims=True)
    yc = y - mu
    var = jnp.mean(yc * yc, axis=-1, keepdims=True)
    return yc * lax.rsqrt(var + LN_EPS) * g + b


def _dot_nt(a, b):
    return lax.dot_general(a, b, (((1,), (1,)), ((), ())), preferred_element_type=F32)


def _pool_kernel(x_ref, halo_ref, w_ref, sc_ref, g_ref, b_ref, o_ref, *, tq, start, first_is_zero):
    i = pl.program_id(1)
    x = x_ref[0]
    halo = halo_ref[0]
    if first_is_zero:
        halo = jnp.where(i == 0, 0.0, halo)
    xa = jnp.concatenate([halo, x], axis=0)
    pos = start + i * tq + lax.broadcasted_iota(jnp.int32, (tq, 1), 0)
    outs = []
    for g, w in enumerate(POOL_WINDOWS):
        sl = slice(g * POOL_GROUP, (g + 1) * POOL_GROUP)
        s = xa[:, sl]
        sh = 1
        while sh < w:
            s = s + pltpu.roll(s, sh, axis=0)
            sh *= 2
        cnt = jnp.minimum(w, pos + 1).astype(F32)
        p = s[HALO:] / cnt - x[:, sl]
        outs.append(jnp.dot(p.astype(BF16), w_ref[g], preferred_element_type=F32))
    y = jnp.concatenate(outs, axis=1) * sc_ref[...]
    o_ref[0] = _ln(ALPHA * x + y, g_ref[...], b_ref[...])


def _pool_layer(x, halo, w_bf, scale, g, b, *, tq, start, first_is_zero):
    B, T, D = x.shape
    nq = T // tq
    if first_is_zero:
        halo_spec = pl.BlockSpec((1, HALO, D), lambda bb, i: (bb, jnp.maximum(i * (tq // HALO) - 1, 0), 0))
    else:
        halo_spec = pl.BlockSpec((1, HALO, D), lambda bb, i: (bb, 0, 0))
    row = lambda: pl.BlockSpec((1, D), lambda bb, i: (0, 0))
    return pl.pallas_call(
        functools.partial(_pool_kernel, tq=tq, start=start, first_is_zero=first_is_zero),
        grid=(B, nq),
        in_specs=[pl.BlockSpec((1, tq, D), lambda bb, i: (bb, i, 0)),
                  halo_spec,
                  pl.BlockSpec((len(POOL_WINDOWS), POOL_GROUP, POOL_GROUP), lambda bb, i: (0, 0, 0)),
                  row(), row(), row()],
        out_specs=pl.BlockSpec((1, tq, D), lambda bb, i: (bb, i, 0)),
        out_shape=jax.ShapeDtypeStruct((B, T, D), F32),
        compiler_params=pltpu.CompilerParams(dimension_semantics=("arbitrary", "arbitrary")),
        name="pool_mix_ln",
    )(x, halo, w_bf, scale, g, b)


def _mlp_kernel(x_ref, w1_ref, w2_ref, g_ref, b_ref, o_ref, *, ffc):
    x = x_ref[...]
    xb = x.astype(BF16)
    acc = jnp.zeros(x.shape, F32)
    for c in range(D_FF // ffc):
        h = jnp.dot(xb, w1_ref[:, c * ffc:(c + 1) * ffc], preferred_element_type=F32)
        h = jnp.maximum(h, 0.0)
        acc = acc + jnp.dot((h * h).astype(BF16), w2_ref[c * ffc:(c + 1) * ffc, :],
                            preferred_element_type=F32)
    o_ref[...] = _ln(ALPHA * x + acc, g_ref[...], b_ref[...])


def _mlp_layer(x2, w1_bf, w2_bf, g, b, *, tm):
    n, D = x2.shape
    row = lambda: pl.BlockSpec((1, D), lambda i: (0, 0))
    return pl.pallas_call(
        functools.partial(_mlp_kernel, ffc=1024),
        grid=(n // tm,),
        in_specs=[pl.BlockSpec((tm, D), lambda i: (i, 0)),
                  pl.BlockSpec((D, D_FF), lambda i: (0, 0), pipeline_mode=pl.Buffered(1)),
                  pl.BlockSpec((D_FF, D), lambda i: (0, 0), pipeline_mode=pl.Buffered(1)),
                  row(), row()],
        out_specs=pl.BlockSpec((tm, D), lambda i: (i, 0)),
        out_shape=jax.ShapeDtypeStruct((n, D), F32),
        compiler_params=pltpu.CompilerParams(dimension_semantics=("arbitrary",),
                                             vmem_limit_bytes=VMEM_LIMIT),
        name="mlp_ln",
    )(x2, w1_bf, w2_bf, g, b)


def _proj_kernel(x_ref, wqkv_ref, wqi_ref, wkw_ref, kng_ref, knb_ref,
                 q_ref, k_ref, v_ref, kb_ref, vb_ref, qi_ref, ki_ref, wt_ref):
    xb = x_ref[...].astype(BF16)
    hq = jnp.dot(xb, wqkv_ref[:, :D_MODEL], preferred_element_type=F32)
    q_ref[...] = (hq * Q_SCALE).astype(BF16)
    for part, (f_ref, b_ref) in enumerate(((k_ref, kb_ref), (v_ref, vb_ref)), start=1):
        h = jnp.dot(xb, wqkv_ref[:, part * D_MODEL:(part + 1) * D_MODEL], preferred_element_type=F32)
        f_ref[...] = h
        b_ref[...] = h.astype(BF16)
    hqi = jnp.dot(xb, wqi_ref[...], preferred_element_type=F32)
    for hh in range(IDX_HEADS):
        qi_ref[hh] = hqi[:, hh * IDX_DIM:(hh + 1) * IDX_DIM].astype(BF16)
    hkw = jnp.dot(xb, wkw_ref[...], preferred_element_type=F32)
    ki_ref[...] = _ln(hkw[:, :IDX_DIM], kng_ref[...], knb_ref[...])
    wt_ref[...] = hkw[:, IDX_DIM:IDX_DIM + IDX_HEADS] * (IDX_HEADS ** -0.5)


def _proj_layer(x2, wqkv_bf, wqi_bf, wkw_bf, kn_g, kn_b, *, tm):
    n, D = x2.shape
    full = lambda shp: pl.BlockSpec(shp, lambda i: tuple(0 for _ in shp))
    rows = lambda c: pl.BlockSpec((tm, c), lambda i: (i, 0))
    sds = jax.ShapeDtypeStruct
    return pl.pallas_call(
        _proj_kernel,
        grid=(n // tm,),
        in_specs=[rows(D), full((D, 3 * D_MODEL)), full((D, D_QI)), full((D, LANES)),
                  full((1, IDX_DIM)), full((1, IDX_DIM))],
        out_specs=[rows(D), rows(D), rows(D), rows(D), rows(D),
                   pl.BlockSpec((IDX_HEADS, tm, IDX_DIM), lambda i: (0, i, 0)),
                   rows(IDX_DIM), rows(IDX_HEADS)],
        out_shape=[sds((n, D), BF16), sds((n, D), F32), sds((n, D), F32), sds((n, D), BF16),
                   sds((n, D), BF16), sds((IDX_HEADS, n, IDX_DIM), BF16),
                   sds((n, IDX_DIM), F32), sds((n, IDX_HEADS), F32)],
        compiler_params=pltpu.CompilerParams(dimension_semantics=("arbitrary",),
                                             vmem_limit_bytes=VMEM_LIMIT),
        name="attn_in_proj",
    )(x2, wqkv_bf, wqi_bf, wkw_bf, kn_g, kn_b)


def _proj_t_kernel(x_ref, wqt_ref, wk_ref, wv_ref, wvt_ref, wqit_ref, wkw_ref, wkwt_ref, kng_ref, knb_ref,
                   qt_ref, k_ref, v_ref, kb_ref, vt_ref, qit_ref, ki_ref, wtt_ref):
    xb = x_ref[...].astype(BF16)
    qt_ref[...] = (_dot_nt(wqt_ref[...], xb) * Q_SCALE).astype(BF16)
    hk = jnp.dot(xb, wk_ref[...], preferred_element_type=F32)
    k_ref[...] = hk
    kb_ref[...] = hk.astype(BF16)
    v_ref[...] = jnp.dot(xb, wv_ref[...], preferred_element_type=F32)
    vt_ref[...] = _dot_nt(wvt_ref[...], xb).astype(BF16)
    qit_ref[...] = _dot_nt(wqit_ref[...], xb).astype(BF16)
    hkw = jnp.dot(xb, wkw_ref[...], preferred_element_type=F32)
    ki_ref[...] = _ln(hkw[:, :IDX_DIM], kng_ref[...], knb_ref[...])
    hkwt = _dot_nt(wkwt_ref[...], xb)
    wtt_ref[...] = hkwt[IDX_DIM:IDX_DIM + IDX_HEADS, :] * (IDX_HEADS ** -0.5)


def _proj_t_layer(x2, w_in, kn_g, kn_b, *, tm):
    n, D = x2.shape
    wq, wk, wv = (w_in[:, p * D_MODEL:(p + 1) * D_MODEL] for p in range(3))
    wqi = w_in[:, 3 * D_MODEL:3 * D_MODEL + D_QI]
    wkw = jnp.pad(w_in[:, 3 * D_MODEL + D_QI:], ((0, 0), (0, LANES - IDX_DIM - IDX_HEADS)))
    bf = lambda a: a.astype(BF16)
    weights = [bf(wq.T), bf(wk), bf(wv), bf(wv.T), bf(wqi.T), bf(wkw), bf(wkw.T)]
    full = lambda shp: pl.BlockSpec(shp, lambda i: tuple(0 for _ in shp))
    rows = lambda c: pl.BlockSpec((tm, c), lambda i: (i, 0))
    cols = lambda r: pl.BlockSpec((r, tm), lambda i: (0, i))
    sds = jax.ShapeDtypeStruct
    return pl.pallas_call(
        _proj_t_kernel,
        grid=(n // tm,),
        in_specs=[rows(D)] + [full(w.shape) for w in weights] + [full((1, IDX_DIM)), full((1, IDX_DIM))],
        out_specs=[cols(D), rows(D), rows(D), rows(D), cols(D), cols(D_QI), rows(IDX_DIM), cols(IDX_HEADS)],
        out_shape=[sds((D, n), BF16), sds((n, D), F32), sds((n, D), F32), sds((n, D), BF16),
                   sds((D, n), BF16), sds((D_QI, n), BF16), sds((n, IDX_DIM), F32), sds((IDX_HEADS, n), F32)],
        compiler_params=pltpu.CompilerParams(dimension_semantics=("arbitrary",),
                                             vmem_limit_bytes=VMEM_LIMIT),
        name="attn_in_proj_t",
    )(x2, *weights, kn_g, kn_b)


def _oproj_kernel(o_ref, x_ref, wo_ref, g_ref, b_ref, y_ref):
    h = jnp.dot(o_ref[...], wo_ref[...], preferred_element_type=F32)
    y_ref[...] = _ln(ALPHA * x_ref[...] + h, g_ref[...], b_ref[...])


def _oproj_layer(o2, x2, wo_bf, g, b, *, tm):
    n, D = x2.shape
    row = lambda: pl.BlockSpec((1, D), lambda i: (0, 0))
    return pl.pallas_call(
        _oproj_kernel,
        grid=(n // tm,),
        in_specs=[pl.BlockSpec((tm, D), lambda i: (i, 0)), pl.BlockSpec((tm, D), lambda i: (i, 0)),
                  pl.BlockSpec((D, D), lambda i: (0, 0)), row(), row()],
        out_specs=pl.BlockSpec((tm, D), lambda i: (i, 0)),
        out_shape=jax.ShapeDtypeStruct((n, D), F32),
        compiler_params=pltpu.CompilerParams(dimension_semantics=("arbitrary",)),
        name="attn_out_proj_ln",
    )(o2, x2, wo_bf, g, b)


def _search_threshold(count_ge, count_tie, rmax, rmin, ncols, ksel):
    inf = jnp.inf
    shape = rmax.shape

    def status(flo, fhi, clo):
        mid = 0.5 * flo + 0.5 * fhi
        lo_inf = flo == -inf
        hi_inf = fhi == inf
        p = mid
        p = jnp.where((flo == 0.0) & (fhi > F32_TINY), F32_TINY, p)
        p = jnp.where((flo < 0.0) & (fhi > 0.0), 0.0, p)
        p = jnp.where(lo_inf, jnp.where(fhi > rmin, rmin, -F32_MAX), p)
        p = jnp.where(hi_inf, rmax, p)
        adjacent = ~lo_inf & ~hi_inf & ((mid <= flo) | (mid >= fhi))
        done = (clo == ksel) | adjacent | (hi_inf & (flo >= rmax)) | (lo_inf & (fhi <= -F32_MAX))
        return jnp.where(done, 1.0, 0.0), p

    def search_body(st):
        it, flo, fhi, clo, chi, donef, p, _ = st
        cnt = count_ge(p)
        live = donef < 0.5
        up_lo = live & (cnt >= ksel)
        up_hi = live & (cnt < ksel)
        flo, clo = jnp.where(up_lo, p, flo), jnp.where(up_lo, cnt, clo)
        fhi, chi = jnp.where(up_hi, p, fhi), jnp.where(up_hi, cnt, chi)
        donef, p = status(flo, fhi, clo)
        return it + 1, flo, fhi, clo, chi, donef, p, (jnp.min(donef) > 0.5).astype(jnp.int32)

    flo0 = jnp.full(shape, -inf, F32)
    fhi0 = jnp.full(shape, inf, F32)
    clo0 = jnp.zeros(shape, F32) + jnp.asarray(ncols).astype(F32)
    done0, p0 = status(flo0, fhi0, clo0)
    st = lax.while_loop(lambda st: (st[0] < SEARCH_MAX_STEPS) & (st[7] == 0), search_body,
                        (jnp.int32(0), flo0, fhi0, clo0, jnp.zeros(shape, F32), done0, p0, jnp.int32(0)))
    _, thr, _, clo, chi, _, _, _ = st

    tie_rows = clo > ksel
    need = ksel - chi
    ncols_i = jnp.asarray(ncols).astype(jnp.int32)

    def tie_phase():
        def tb(_, st):
            jlo, jhi = st
            mid = (jlo + jhi) >> 1
            ok = count_tie(thr, mid) >= need
            return jnp.where(ok, jlo, mid), jnp.where(ok, mid, jhi)
        init_j = (jnp.full(shape, -1, jnp.int32), jnp.zeros(shape, jnp.int32) + (ncols_i - 1))
        _, jhi = lax.fori_loop(0, 14, tb, init_j)
        return jnp.where(tie_rows, jhi, ncols_i)

    any_tie = jnp.max(jnp.where(tie_rows, 1.0, 0.0)) > 0.0
    jcut = lax.cond(any_tie, tie_phase, lambda: jnp.zeros(shape, jnp.int32) + ncols_i)
    return thr, jcut


SUBLANES = 8
NACC = 4


def _select_bias_t(sc_ref, n_chunks, cw, tpos, ksel):
    nq = sc_ref.shape[1]
    grp = cw // SUBLANES
    inf = jnp.inf
    view = (grp // NACC, NACC, SUBLANES, nq)
    krow = ((lax.broadcasted_iota(jnp.int32, view, 0) * NACC + lax.broadcasted_iota(jnp.int32, view, 1)) * SUBLANES
            + lax.broadcasted_iota(jnp.int32, view, 2))

    def tile(v):
        return jnp.broadcast_to(v, (SUBLANES, nq))[None, None]

    def fold(tile_fn, init):
        def body(c, accs):
            rows = pl.ds(pl.multiple_of(c * cw, cw), cw)
            return tile_fn(accs, sc_ref[rows, :].reshape(view), c * cw)
        return lax.fori_loop(0, n_chunks, body, init)

    def finish(acc, op):
        return op(op(acc, axis=0), axis=0, keepdims=True)

    def count(pred):
        acc = fold(lambda a, x, base: a + jnp.sum(pred(x, base), axis=0), jnp.zeros(view[1:], F32))
        return finish(acc, jnp.sum)

    mx, mn = fold(lambda a, x, base: (jnp.maximum(a[0], jnp.max(x, axis=0)),
                                      jnp.minimum(a[1], jnp.min(jnp.where(x == -inf, inf, x), axis=0))),
                  (jnp.full(view[1:], -inf, F32), jnp.full(view[1:], inf, F32)))
    rmax = finish(mx, jnp.max)
    rmin = finish(mn, jnp.min)

    def count_ge(p):
        pb = tile(p)
        return count(lambda x, base: jnp.where(x >= pb, 1.0, 0.0))

    def count_tie(t, j):
        tb, jb = tile(t), tile(j)
        return count(lambda x, base: jnp.where(x == tb, jnp.where(krow + base <= jb, 1.0, 0.0), 0.0))

    thr, jcut = _search_threshold(count_ge, count_tie, rmax, rmin, n_chunks * cw, ksel)
    tb, jb = tile(thr), tile(jnp.minimum(jcut, tpos))

    def write(c, carry):
        rows = pl.ds(pl.multiple_of(c * cw, cw), cw)
        x = sc_ref[rows, :].reshape(view)
        tie_sel = jnp.where(krow + c * cw <= jb, 0.0, NEG_BIAS)
        bias = jnp.where(x > tb, 0.0, jnp.where(x == tb, tie_sel, NEG_BIAS))
        sc_ref[rows, :] = bias.reshape(cw, nq)
        return carry
    lax.fori_loop(0, n_chunks, write, 0)


def _select_bias(sc_ref, r0, nr, n_chunks, cw, tpos, ksel):
    sub = min(nr, SEL_SUB)
    nsub = nr // sub
    ncols = n_chunks * cw
    nt = cw // LANES
    lane = lax.broadcasted_iota(jnp.int32, (sub, LANES), 1)
    inf = jnp.inf

    def fold(tile_fn, init, cols=(), store=False):
        res = []
        for r in range(nsub):
            rows = slice(r0 + r * sub, r0 + (r + 1) * sub)
            bc = [jnp.broadcast_to(c[:, r * sub:(r + 1) * sub], (LANES, sub)).T for c in cols]

            def body(c, accs, rows=rows, bc=bc):
                cs = pl.ds(pl.multiple_of(c * cw, cw), cw)
                x = sc_ref[rows, cs]
                outs = []
                for j in range(nt):
                    accs = tile_fn(accs, x[:, j * LANES:(j + 1) * LANES], c * cw + j * LANES, bc)
                    if store:
                        outs.append(accs)
                if store:
                    sc_ref[rows, cs] = jnp.concatenate(outs, axis=1) if nt > 1 else outs[0]
                    return 0
                return accs
            res.append(lax.fori_loop(0, n_chunks, body, init))
        return res

    def row_reduce(parts, k, op):
        return jnp.concatenate([op(p[k].T, axis=0, keepdims=True) for p in parts], axis=1)

    def count(pred, cols):
        parts = fold(lambda a, x, base, bc: (a[0] + pred(x, base, bc),), (jnp.zeros((sub, LANES), F32),), cols)
        return row_reduce(parts, 0, jnp.sum)

    parts = fold(lambda a, x, base, bc: (jnp.maximum(a[0], x), jnp.minimum(a[1], jnp.where(x == -inf, inf, x))),
                 (jnp.full((sub, LANES), -inf, F32), jnp.full((sub, LANES), inf, F32)))
    rmax = row_reduce(parts, 0, jnp.max)
    rmin = row_reduce(parts, 1, jnp.min)

    thr, jcut = _search_threshold(
        lambda p: count(lambda x, base, bc: jnp.where(x >= bc[0], 1.0, 0.0), (p,)),
        lambda t, j: count(lambda x, base, bc: jnp.where(x == bc[0], jnp.where(lane + base <= bc[1], 1.0, 0.0), 0.0),
                           (t, j)),
        rmax, rmin, ncols, ksel)

    def bias_tile(_, x, base, bc):
        tie_sel = jnp.where(lane + base <= bc[1], 0.0, NEG_BIAS)
        return jnp.where(x > bc[0], 0.0, jnp.where(x == bc[0], tie_sel, NEG_BIAS))
    fold(bias_tile, 0, (thr, jnp.minimum(jcut, tpos)), store=True)


SEL_ROWS = 256


def _attn_prompt_kernel(qmap_ref, kmap_ref, q_ref, qi_ref, wt_ref, ki_ref, k_ref, v_ref, o_ref,
                        sc_ref, wb_ref, m_ref, l_ref, a_ref, acc_ref, s_ref, p_ref, *, tq, tk, ksel):
    n = pl.program_id(1)
    i = qmap_ref[n]
    kj = kmap_ref[n]
    last = ((i + 1) * tq - 1) // tk

    @pl.when(kj == 0)
    def _():
        wts = wt_ref[0] * (IDX_DIM ** -0.5)
        for h in range(IDX_HEADS):
            wb_ref[h] = jnp.broadcast_to(wts[:, h:h + 1], (tq, LANES))
        tpos = i * tq + lax.broadcasted_iota(jnp.int32, (tq, 1), 0)

        def chunk_scores(c, masked):
            cols = pl.ds(pl.multiple_of(c * tk, tk), tk)
            kic = ki_ref[0, cols, :].astype(BF16)
            sc = None
            for h in range(IDX_HEADS):
                d = jnp.maximum(_dot_nt(qi_ref[h, 0], kic), 0.0)
                d = d * jnp.concatenate([wb_ref[h]] * (tk // LANES), axis=1)
                sc = d if sc is None else sc + d
            if masked:
                kidx = c * tk + lax.broadcasted_iota(jnp.int32, (tq, tk), 1)
                sc = jnp.where(kidx <= tpos, sc, -jnp.inf)
            sc_ref[:, cols] = sc

        def body(c, carry):
            chunk_scores(c, False)
            return carry
        lax.fori_loop(0, last, body, 0)
        chunk_scores(last, True)

        for r in range(tq // SEL_ROWS):
            tpos_row = i * tq + r * SEL_ROWS + lax.broadcasted_iota(jnp.int32, (1, SEL_ROWS), 1)
            _select_bias(sc_ref, r * SEL_ROWS, SEL_ROWS, last + 1, tk, tpos_row, ksel)
        m_ref[...] = jnp.full(m_ref.shape, NEG_BIAS, F32)
        l_ref[...] = jnp.zeros(l_ref.shape, F32)
        acc_ref[...] = jnp.zeros(acc_ref.shape, F32)

    cols = pl.ds(pl.multiple_of(kj * tk, tk), tk)
    heads = [slice(h * HEAD_DIM, (h + 1) * HEAD_DIM) for h in range(N_HEADS)]
    rep = tk // LANES
    for h, hs in enumerate(heads):
        s = _dot_nt(q_ref[0, :, hs], k_ref[0, :, hs]) + sc_ref[:, cols]
        s_ref[h] = s
        m_old = m_ref[h]
        m_new = jnp.maximum(m_old, jnp.broadcast_to(jnp.max(s, axis=1, keepdims=True), (tq, LANES)))
        a_ref[h] = jnp.exp2(m_old - m_new)
        m_ref[h] = m_new
    for h, hs in enumerate(heads):
        p = jnp.exp2(s_ref[h] - jnp.concatenate([m_ref[h]] * rep, axis=1))
        l_ref[h] = a_ref[h][:, :1] * l_ref[h] + jnp.sum(p, axis=1, keepdims=True)
        p_ref[h] = p.astype(BF16)
    for h, hs in enumerate(heads):
        acc_ref[:, hs] = a_ref[h] * acc_ref[:, hs] + jnp.dot(p_ref[h], v_ref[0, :, hs],
                                                             preferred_element_type=F32)

    @pl.when(kj == last)
    def _():
        for h in range(N_HEADS):
            hs = slice(h * HEAD_DIM, (h + 1) * HEAD_DIM)
            o_ref[0, :, hs] = (acc_ref[:, hs] / l_ref[h]).astype(BF16)


def _attn_prompt(q, qi, wt, ki, kb, vb, *, tq=256, tk=512):
    B, T, D = q.shape
    ksel = min(TOPK_MAX, T // 4)
    nq = T // tq
    qmap, kmap = [], []
    for i in range(nq):
        for kj in range(((i + 1) * tq - 1) // tk + 1):
            qmap.append(i)
            kmap.append(kj)
    qmap = jnp.asarray(np.asarray(qmap, np.int32))
    kmap = jnp.asarray(np.asarray(kmap, np.int32))
    grid_spec = pltpu.PrefetchScalarGridSpec(
        num_scalar_prefetch=2,
        grid=(B, int(qmap.shape[0])),
        in_specs=[pl.BlockSpec((1, tq, D), lambda b, n, qm, km: (b, qm[n], 0)),
                  pl.BlockSpec((IDX_HEADS, 1, tq, IDX_DIM), lambda b, n, qm, km: (0, b, qm[n], 0)),
                  pl.BlockSpec((1, tq, IDX_HEADS), lambda b, n, qm, km: (b, qm[n], 0)),
                  pl.BlockSpec((1, T, IDX_DIM), lambda b, n, qm, km: (b, 0, 0)),
                  pl.BlockSpec((1, tk, D), lambda b, n, qm, km: (b, km[n], 0)),
                  pl.BlockSpec((1, tk, D), lambda b, n, qm, km: (b, km[n], 0))],
        out_specs=pl.BlockSpec((1, tq, D), lambda b, n, qm, km: (b, qm[n], 0)),
        scratch_shapes=[pltpu.VMEM((tq, T), F32),
                        pltpu.VMEM((IDX_HEADS, tq, LANES), F32),
                        pltpu.VMEM((N_HEADS, tq, LANES), F32),
                        pltpu.VMEM((N_HEADS, tq, 1), F32),
                        pltpu.VMEM((N_HEADS, tq, LANES), F32),
                        pltpu.VMEM((tq, D), F32),
                        pltpu.VMEM((N_HEADS, tq, tk), F32),
                        pltpu.VMEM((N_HEADS, tq, tk), BF16)])
    return pl.pallas_call(
        functools.partial(_attn_prompt_kernel, tq=tq, tk=tk, ksel=ksel),
        grid_spec=grid_spec,
        out_shape=jax.ShapeDtypeStruct((B, T, D), BF16),
        compiler_params=pltpu.CompilerParams(dimension_semantics=("arbitrary", "arbitrary"),
                                             vmem_limit_bytes=VMEM_LIMIT),
        name="attn_prompt",
    )(qmap, kmap, q, qi, wt, ki, kb, vb)


def _attn_prompt_t_kernel(qmap_ref, kmap_ref, qt_ref, qit_ref, wtt_ref, ki_ref, k_ref, vt_ref, o_ref,
                          sc_ref, wb_ref, m_ref, l_ref, a_ref, acc_ref, s_ref, p_ref, *, tq, tk, ksel):
    n = pl.program_id(1)
    i = qmap_ref[n]
    kj = kmap_ref[n]
    last = ((i + 1) * tq - 1) // tk
    grp = tk // SUBLANES

    def keyred(x, op):
        return op(op(x.reshape(grp // NACC, NACC, SUBLANES, tq), axis=0), axis=0)

    def rep(v8, rows):
        return jnp.concatenate([v8] * (rows // SUBLANES), axis=0)

    @pl.when(kj == 0)
    def _():
        wts = wtt_ref[...] * (IDX_DIM ** -0.5)
        for h in range(IDX_HEADS):
            wb_ref[h] = jnp.broadcast_to(wts[h:h + 1, :], (SUBLANES, tq))
        tpos = i * tq + lax.broadcasted_iota(jnp.int32, (1, tq), 1)

        def chunk_scores(c, masked):
            rows = pl.ds(pl.multiple_of(c * tk, tk), tk)
            kic = ki_ref[rows, :].astype(BF16)
            sc = None
            for h in range(IDX_HEADS):
                d = jnp.dot(kic, qit_ref[h * IDX_DIM:(h + 1) * IDX_DIM, :], preferred_element_type=F32)
                d = jnp.maximum(d, 0.0) * rep(wb_ref[h], tk)
                sc = d if sc is None else sc + d
            if masked:
                kidx = c * tk + lax.broadcasted_iota(jnp.int32, (tk, tq), 0)
                sc = jnp.where(kidx <= tpos, sc, -jnp.inf)
            sc_ref[rows, :] = sc

        def body(c, carry):
            chunk_scores(c, False)
            return carry
        lax.fori_loop(0, last, body, 0)
        chunk_scores(last, True)

        _select_bias_t(sc_ref, last + 1, tk, tpos, ksel)
        m_ref[...] = jnp.full(m_ref.shape, NEG_BIAS, F32)
        l_ref[...] = jnp.zeros(l_ref.shape, F32)
        acc_ref[...] = jnp.zeros(acc_ref.shape, F32)

    rows = pl.ds(pl.multiple_of(kj * tk, tk), tk)
    heads = [slice(h * HEAD_DIM, (h + 1) * HEAD_DIM) for h in range(N_HEADS)]
    for h, hs in enumerate(heads):
        s = jnp.dot(k_ref[:, hs], qt_ref[hs, :], preferred_element_type=F32) + sc_ref[rows, :]
        s_ref[h] = s
        m_old = m_ref[h]
        smax = jnp.max(keyred(s, jnp.max), axis=0, keepdims=True)
        m_new = jnp.maximum(m_old, jnp.broadcast_to(smax, (SUBLANES, tq)))
        a_ref[h] = jnp.exp2(m_old - m_new)
        m_ref[h] = m_new
    for h, hs in enumerate(heads):
        p = jnp.exp2(s_ref[h] - rep(m_ref[h], tk))
        l_ref[h] = a_ref[h] * l_ref[h] + keyred(p, jnp.sum)
        p_ref[h] = p.astype(BF16)
    for h, hs in enumerate(heads):
        acc_ref[h] = rep(a_ref[h], HEAD_DIM) * acc_ref[h] + jnp.dot(vt_ref[hs, :], p_ref[h],
                                                                     preferred_element_type=F32)

    @pl.when(kj == last)
    def _():
        for h, hs in enumerate(heads):
            l = jnp.broadcast_to(jnp.sum(l_ref[h], axis=0, keepdims=True), (SUBLANES, tq))
            o_ref[:, hs] = (acc_ref[h] / rep(l, HEAD_DIM)).T.astype(BF16)


def _attn_prompt_t(B, qt, qit, wtt, ki, kb, vt, *, tq=256, tk=512):
    n, D = kb.shape
    T = n // B
    ksel = min(TOPK_MAX, T // 4)
    nq, nk = T // tq, T // tk
    qmap, kmap = [], []
    for i in range(nq):
        for kj in range(((i + 1) * tq - 1) // tk + 1):
            qmap.append(i)
            kmap.append(kj)
    qmap = jnp.asarray(np.asarray(qmap, np.int32))
    kmap = jnp.asarray(np.asarray(kmap, np.int32))
    qcol = lambda r: pl.BlockSpec((r, tq), lambda b, s, qm, km: (0, b * nq + qm[s]))
    grid_spec = pltpu.PrefetchScalarGridSpec(
        num_scalar_prefetch=2,
        grid=(B, int(qmap.shape[0])),
        in_specs=[qcol(D), qcol(D_QI), qcol(IDX_HEADS),
                  pl.BlockSpec((T, IDX_DIM), lambda b, s, qm, km: (b, 0)),
                  pl.BlockSpec((tk, D), lambda b, s, qm, km: (b * nk + km[s], 0)),
                  pl.BlockSpec((D, tk), lambda b, s, qm, km: (0, b * nk + km[s]))],
        out_specs=pl.BlockSpec((tq, D), lambda b, s, qm, km: (b * nq + qm[s], 0)),
        scratch_shapes=[pltpu.VMEM((T, tq), F32),
                        pltpu.VMEM((IDX_HEADS, SUBLANES, tq), F32),
                        pltpu.VMEM((N_HEADS, SUBLANES, tq), F32),
                        pltpu.VMEM((N_HEADS, SUBLANES, tq), F32),
                        pltpu.VMEM((N_HEADS, SUBLANES, tq), F32),
                        pltpu.VMEM((N_HEADS, HEAD_DIM, tq), F32),
                        pltpu.VMEM((N_HEADS, tk, tq), F32),
                        pltpu.VMEM((N_HEADS, tk, tq), BF16)])
    return pl.pallas_call(
        functools.partial(_attn_prompt_t_kernel, tq=tq, tk=tk, ksel=ksel),
        grid_spec=grid_spec,
        out_shape=jax.ShapeDtypeStruct((n, D), BF16),
        compiler_params=pltpu.CompilerParams(dimension_semantics=("arbitrary", "arbitrary"),
                                             vmem_limit_bytes=VMEM_LIMIT),
        name="attn_prompt_t",
    )(qmap, kmap, qt, qit, wtt, ki, kb, vt)


KI_PAGES_PER_STEP = 16
KV_PAGES_PER_STEP = 8
SEL_CHUNK = 5 * LANES


def _sample_scores_kernel(pt_ref, qi_ref, wt_ref, kin_ref, *rest, n_pages, tq):
    del pt_ref
    g1 = KI_PAGES_PER_STEP
    kidx_refs, sc_ref = rest[:g1], rest[g1]
    s = pl.program_id(1)
    wcol = jnp.broadcast_to(wt_ref[0] * (IDX_DIM ** -0.5), (IDX_HEADS * tq, PAGE_SIZE))

    def chunk_scores(kit_chunk):
        d = jnp.maximum(jnp.dot(qi_ref[0], kit_chunk.astype(BF16), preferred_element_type=F32), 0.0) * wcol
        sc = d[0:tq]
        for h in range(1, IDX_HEADS):
            sc = sc + d[h * tq:(h + 1) * tq]
        return sc

    for g in range(g1):
        page = s * g1 + g
        sc_ref[0, :, pl.ds(pl.multiple_of(page * PAGE_SIZE, PAGE_SIZE), PAGE_SIZE)] = chunk_scores(kidx_refs[g][0, 0])

    @pl.when(s == n_pages // g1 - 1)
    def _():
        qrow = lax.broadcasted_iota(jnp.int32, (tq, PAGE_SIZE), 0)
        jcol = lax.broadcasted_iota(jnp.int32, (tq, PAGE_SIZE), 1)
        sc_ref[0, :, n_pages * PAGE_SIZE:] = jnp.where(jcol <= qrow, chunk_scores(kin_ref[0]), -jnp.inf)


def _sample_scores(layer, page_table, qi, wt, ki_new, cache_kidx, *, tq):
    Bd, n_pages = page_table.shape
    g1 = KI_PAGES_PER_STEP
    ncols = (n_pages + 1) * PAGE_SIZE

    def per_b(shape):
        return pl.BlockSpec((1,) + shape, lambda b, s, pt: (b,) + tuple(0 for _ in shape))

    def kidx_spec(g):
        return pl.BlockSpec((1, 1, IDX_DIM, PAGE_SIZE), lambda b, s, pt: (layer, pt[b, s * g1 + g], 0, 0))

    grid_spec = pltpu.PrefetchScalarGridSpec(
        num_scalar_prefetch=1,
        grid=(Bd, n_pages // g1),
        in_specs=[per_b((IDX_HEADS * tq, IDX_DIM)), per_b((IDX_HEADS * tq, 1)), per_b((IDX_DIM, PAGE_SIZE))]
                 + [kidx_spec(g) for g in range(g1)],
        out_specs=per_b((tq, ncols)))
    return pl.pallas_call(
        functools.partial(_sample_scores_kernel, n_pages=n_pages, tq=tq),
        grid_spec=grid_spec,
        out_shape=jax.ShapeDtypeStruct((Bd, tq, ncols), F32),
        compiler_params=pltpu.CompilerParams(dimension_semantics=("arbitrary", "arbitrary")),
        name="sample_scores",
    )(page_table, qi, wt, ki_new, *([cache_kidx] * g1))


def _sample_select_kernel(sc_ref, o_ref, *, tq, past, ksel):
    o_ref[...] = sc_ref[...]
    nr, ncols = o_ref.shape
    tpos = past + lax.rem(lax.broadcasted_iota(jnp.int32, (1, nr), 1), tq)
    _select_bias(o_ref, 0, nr, ncols // SEL_CHUNK, SEL_CHUNK, tpos, ksel)


def _sample_select(sc2, *, tq, past, ksel):
    n, ncols = sc2.shape
    nr = min(SEL_ROWS, n)
    return pl.pallas_call(
        functools.partial(_sample_select_kernel, tq=tq, past=past, ksel=ksel),
        grid=(n // nr,),
        in_specs=[pl.BlockSpec((nr, ncols), lambda i: (i, 0))],
        out_specs=pl.BlockSpec((nr, ncols), lambda i: (i, 0)),
        out_shape=jax.ShapeDtypeStruct((n, ncols), F32),
        compiler_params=pltpu.CompilerParams(dimension_semantics=("arbitrary",), vmem_limit_bytes=VMEM_LIMIT),
        name="sample_select",
    )(sc2)


def _sample_attend_kernel(pt_ref, q_ref, bias_ref, kn_ref, vn_ref, *rest, n_pages, tq):
    del pt_ref
    g2 = KV_PAGES_PER_STEP
    kpage_refs, vpage_refs = rest[:g2], rest[g2:2 * g2]
    o_ref, s_ref, l_ref, acc_ref = rest[2 * g2:]
    nsk = n_pages // g2
    s = pl.program_id(1)
    new_cols = slice(n_pages * PAGE_SIZE, (n_pages + 1) * PAGE_SIZE)

    def head_rows(h):
        return slice(h * tq, (h + 1) * tq)

    def head_cols(h):
        return slice(h * HEAD_DIM, (h + 1) * HEAD_DIM)

    @pl.when(s < nsk)
    def _():
        for g in range(g2):
            cols = pl.ds(pl.multiple_of((s * g2 + g) * PAGE_SIZE, PAGE_SIZE), PAGE_SIZE)
            bias = bias_ref[0, :, cols]
            for h in range(N_HEADS):
                kh = kpage_refs[g][0, 0, pl.ds(h, PAGE_SIZE, stride=N_HEADS), :].astype(BF16)
                s_ref[head_rows(h), cols] = _dot_nt(q_ref[0, :, head_cols(h)], kh) + bias

    @pl.when(s == nsk - 1)
    def _():
        bias = bias_ref[0, :, new_cols]
        for h in range(N_HEADS):
            s_ref[head_rows(h), new_cols] = _dot_nt(q_ref[0, :, head_cols(h)], kn_ref[0, :, head_cols(h)]) + bias
        sall = s_ref[...]
        p = jnp.exp2(sall - jnp.max(sall, axis=1, keepdims=True))
        l_ref[...] = jnp.sum(p, axis=1, keepdims=True)
        s_ref[...] = p
        acc_ref[...] = jnp.zeros(acc_ref.shape, F32)

    @pl.when(s >= nsk)
    def _():
        for h in range(N_HEADS):
            o = jnp.zeros((tq, HEAD_DIM), F32)
            for g in range(g2):
                cols = pl.ds(pl.multiple_of(((s - nsk) * g2 + g) * PAGE_SIZE, PAGE_SIZE), PAGE_SIZE)
                vh = vpage_refs[g][0, 0, pl.ds(h, PAGE_SIZE, stride=N_HEADS), :].astype(BF16)
                o = o + jnp.dot(s_ref[head_rows(h), cols].astype(BF16), vh, preferred_element_type=F32)
            acc_ref[:, head_cols(h)] += o

    @pl.when(s == 2 * nsk - 1)
    def _():
        for h in range(N_HEADS):
            o = acc_ref[:, head_cols(h)] + jnp.dot(s_ref[head_rows(h), new_cols].astype(BF16),
                                                   vn_ref[0, :, head_cols(h)], preferred_element_type=F32)
            o_ref[0, :, head_cols(h)] = (o / l_ref[head_rows(h), :]).astype(BF16)


def _sample_attend(layer, page_table, q, bias, kb_new, vb_new, cache_k, cache_v):
    Bd, tq, D = q.shape
    n_pages = page_table.shape[1]
    ncols = (n_pages + 1) * PAGE_SIZE
    g2 = KV_PAGES_PER_STEP
    nsk = n_pages // g2

    def per_b(shape):
        return pl.BlockSpec((1,) + shape, lambda b, s, pt: (b,) + tuple(0 for _ in shape))

    def page_spec(g, first_step):
        return pl.BlockSpec((1, 1, PAGE_SIZE * N_HEADS, HEAD_DIM),
                            lambda b, s, pt: (layer, pt[b, jnp.clip(s - first_step, 0, nsk - 1) * g2 + g], 0, 0))

    grid_spec = pltpu.PrefetchScalarGridSpec(
        num_scalar_prefetch=1,
        grid=(Bd, 2 * nsk),
        in_specs=[per_b((tq, D)), per_b((tq, ncols)), per_b((PAGE_SIZE, D)), per_b((PAGE_SIZE, D))]
                 + [page_spec(g, 0) for g in range(g2)] + [page_spec(g, nsk) for g in range(g2)],
        out_specs=per_b((tq, D)),
        scratch_shapes=[pltpu.VMEM((N_HEADS * tq, ncols), F32),
                        pltpu.VMEM((N_HEADS * tq, 1), F32),
                        pltpu.VMEM((tq, D), F32)])
    return pl.pallas_call(
        functools.partial(_sample_attend_kernel, n_pages=n_pages, tq=tq),
        grid_spec=grid_spec,
        out_shape=jax.ShapeDtypeStruct((Bd, tq, D), BF16),
        compiler_params=pltpu.CompilerParams(dimension_semantics=("arbitrary", "arbitrary"),
                                             vmem_limit_bytes=VMEM_LIMIT),
        name="sample_attend",
    )(page_table, q, bias, kb_new, vb_new, *([cache_k] * g2), *([cache_v] * g2))


def _pad_rows(a, n):
    return jnp.pad(a, ((0, 0), (0, n - a.shape[1]), (0, 0)))


def kernel(x_prompt, x_sample, state_pool, cache_k, cache_v, cache_kidx, page_table, pool_w, pool_scale,
           attn_w_in, attn_kn_g, attn_kn_b, attn_w_o, mlp_w1, mlp_w2, ln_g, ln_b):
    B, T, D = x_prompt.shape
    Bd, Td, _ = x_sample.shape
    n_attn, n_phys = cache_k.shape[0], cache_k.shape[1]
    n_pages = page_table.shape[1]
    past = n_pages * PAGE_SIZE
    xp, xs = x_prompt, x_sample
    ck = cache_k.reshape(n_attn, n_phys, PAGE_SIZE * N_HEADS, HEAD_DIM)
    cv = cache_v.reshape(n_attn, n_phys, PAGE_SIZE * N_HEADS, HEAD_DIM)
    ckit = cache_kidx.transpose(0, 1, 3, 2)
    pool_p, pool_s = [], []
    kp, vp, kip, ksm, vsm, kism = [], [], [], [], [], []
    for i in range(DEPTH):
        j = i // 2
        g0, b0 = ln_g[i, 0][None], ln_b[i, 0][None]
        g1, b1 = ln_g[i, 1][None], ln_b[i, 1][None]
        if i % 2 == 0:
            w_bf = pool_w[j].astype(BF16)
            scale = pool_scale[j][None]
            pool_p.append(xp[:, T - POOL_BUF:])
            halo_s = jnp.concatenate([jnp.zeros((Bd, 1, D), xs.dtype), state_pool[j].astype(xs.dtype)], axis=1)
            pool_s.append(jnp.concatenate([halo_s, xs], axis=1)[:, -POOL_BUF:])
            xp = _pool_layer(xp, xp, w_bf, scale, g0, b0, tq=512, start=0, first_is_zero=True)
            xs = _pool_layer(xs, halo_s, w_bf, scale, g0, b0, tq=Td, start=past, first_is_zero=False)
        else:
            w_in = attn_w_in[j]
            wqkv = w_in[:, :3 * D_MODEL].astype(BF16)
            wqi = w_in[:, 3 * D_MODEL:3 * D_MODEL + D_QI].astype(BF16)
            wkw = jnp.pad(w_in[:, 3 * D_MODEL + D_QI:], ((0, 0), (0, LANES - IDX_DIM - IDX_HEADS))).astype(BF16)
            kng, knb = attn_kn_g[j][None], attn_kn_b[j][None]
            wo = attn_w_o[j].astype(BF16)

            qt, k, v, kb, vt, qit, ki, wtt = _proj_t_layer(xp.reshape(B * T, D), w_in, kng, knb, tm=512)
            o = _attn_prompt_t(B, qt, qit, wtt, ki, kb, vt)
            xp = _oproj_layer(o, xp.reshape(B * T, D), wo, g0, b0, tm=512).reshape(B, T, D)
            kp.append(k.reshape(B, T, N_HEADS, HEAD_DIM))
            vp.append(v.reshape(B, T, N_HEADS, HEAD_DIM))
            kip.append(ki.reshape(B, T, IDX_DIM))

            n = Bd * Td
            q, k, v, kb, vb, qi, ki, wt = _proj_layer(xs.reshape(n, D), wqkv, wqi, wkw, kng, knb, tm=n)
            qi_s = qi.reshape(IDX_HEADS, Bd, Td, IDX_DIM).transpose(1, 0, 2, 3).reshape(Bd, IDX_HEADS * Td, IDX_DIM)
            wt_s = wt.reshape(Bd, Td, IDX_HEADS).transpose(0, 2, 1).reshape(Bd, IDX_HEADS * Td, 1)
            kit_new = _pad_rows(ki.reshape(Bd, Td, IDX_DIM), PAGE_SIZE).transpose(0, 2, 1)
            sc = _sample_scores(j, page_table, qi_s, wt_s, kit_new, ckit, tq=Td)
            bias = _sample_select(sc.reshape(n, sc.shape[-1]), tq=Td, past=past,
                                  ksel=min(TOPK_MAX, (past + Td) // 4))
            o = _sample_attend(j, page_table, q.reshape(Bd, Td, D), bias.reshape(Bd, Td, -1),
                               _pad_rows(kb.reshape(Bd, Td, D), PAGE_SIZE),
                               _pad_rows(vb.reshape(Bd, Td, D), PAGE_SIZE), ck, cv)
            xs = _oproj_layer(o.reshape(n, D), xs.reshape(n, D), wo, g0, b0, tm=n).reshape(Bd, Td, D)
            ksm.append(k.reshape(Bd, Td, N_HEADS, HEAD_DIM))
            vsm.append(v.reshape(Bd, Td, N_HEADS, HEAD_DIM))
            kism.append(ki.reshape(Bd, Td, IDX_DIM))
        w1, w2 = mlp_w1[i].astype(BF16), mlp_w2[i].astype(BF16)
        xp = _mlp_layer(xp.reshape(B * T, D), w1, w2, g1, b1, tm=512).reshape(B, T, D)
        xs = _mlp_layer(xs.reshape(Bd * Td, D), w1, w2, g1, b1, tm=Bd * Td).reshape(Bd, Td, D)
    return (xp, xs, jnp.stack(pool_p), jnp.stack(pool_s), jnp.stack(kp), jnp.stack(vp), jnp.stack(kip),
            jnp.stack(ksm), jnp.stack(vsm), jnp.stack(kism))
```

```python
import functools
import math

import numpy as np
import jax
import jax.numpy as jnp
from jax import lax
from jax.experimental import pallas as pl
from jax.experimental.pallas import tpu as pltpu

D_MODEL = 1024
DEPTH = 4
PAST_LEN = 8192
PAGE_SIZE = 128
POOL_WINDOWS = (2, 4, 8, 16)
POOL_GROUP = D_MODEL // len(POOL_WINDOWS)
POOL_BUF = max(POOL_WINDOWS) - 1
HALO = POOL_BUF + 1
N_HEADS = 8
HEAD_DIM = D_MODEL // N_HEADS
IDX_HEADS = 8
IDX_DIM = 64
TOPK_MAX = 256
D_FF = 4 * D_MODEL
ALPHA = (2 * DEPTH) ** 0.25
LN_EPS = 1e-5
D_QI = IDX_HEADS * IDX_DIM
Q_SCALE = HEAD_DIM ** -0.5 * math.log2(math.e)

LANES = 128
NEG_BIAS = -1e30
F32_MAX = float(np.finfo(np.float32).max)
F32_TINY = float(np.finfo(np.float32).tiny)
VMEM_LIMIT = 52 * 1024 * 1024
SEARCH_MAX_STEPS = 400
SEL_SUB = 128

F32 = jnp.float32
BF16 = jnp.bfloat16


def _ln(y, g, b):
    mu = jnp.mean(y, axis=-1, keepdims=True)
    yc = y - mu
    var = jnp.mean(yc * yc, axis=-1, keepdims=True)
    return yc * lax.rsqrt(var + LN_EPS) * g + b


def _dot_nt(a, b):
    return lax.dot_general(a, b, (((1,), (1,)), ((), ())), preferred_element_type=F32)


def _pool_kernel(x_ref, halo_ref, w_ref, sc_ref, g_ref, b_ref, o_ref, *, tq, start, first_is_zero):
    i = pl.program_id(1)
    x = x_ref[0]
    halo = halo_ref[0]
    if first_is_zero:
        halo = jnp.where(i == 0, 0.0, halo)
    xa = jnp.concatenate([halo, x], axis=0)
    pos = start + i * tq + lax.broadcasted_iota(jnp.int32, (tq, 1), 0)
    outs = []
    for g, w in enumerate(POOL_WINDOWS):
        sl = slice(g * POOL_GROUP, (g + 1) * POOL_GROUP)
        s = xa[:, sl]
        sh = 1
        while sh < w:
            s = s + pltpu.roll(s, sh, axis=0)
            sh *= 2
        cnt = jnp.minimum(w, pos + 1).astype(F32)
        p = s[HALO:] / cnt - x[:, sl]
        outs.append(jnp.dot(p.astype(BF16), w_ref[g], preferred_element_type=F32))
    y = jnp.concatenate(outs, axis=1) * sc_ref[...]
    o_ref[0] = _ln(ALPHA * x + y, g_ref[...], b_ref[...])


def _pool_layer(x, halo, w_bf, scale, g, b, *, tq, start, first_is_zero):
    B, T, D = x.shape
    nq = T // tq
    if first_is_zero:
        halo_spec = pl.BlockSpec((1, HALO, D), lambda bb, i: (bb, jnp.maximum(i * (tq // HALO) - 1, 0), 0))
    else:
        halo_spec = pl.BlockSpec((1, HALO, D), lambda bb, i: (bb, 0, 0))
    row = lambda: pl.BlockSpec((1, D), lambda bb, i: (0, 0))
    return pl.pallas_call(
        functools.partial(_pool_kernel, tq=tq, start=start, first_is_zero=first_is_zero),
        grid=(B, nq),
        in_specs=[pl.BlockSpec((1, tq, D), lambda bb, i: (bb, i, 0)),
                  halo_spec,
                  pl.BlockSpec((len(POOL_WINDOWS), POOL_GROUP, POOL_GROUP), lambda bb, i: (0, 0, 0)),
                  row(), row(), row()],
        out_specs=pl.BlockSpec((1, tq, D), lambda bb, i: (bb, i, 0)),
        out_shape=jax.ShapeDtypeStruct((B, T, D), F32),
        compiler_params=pltpu.CompilerParams(dimension_semantics=("arbitrary", "arbitrary")),
        name="pool_mix_ln",
    )(x, halo, w_bf, scale, g, b)


def _mlp_kernel(x_ref, w1_ref, w2_ref, g_ref, b_ref, o_ref, *, ffc):
    x = x_ref[...]
    xb = x.astype(BF16)
    acc = jnp.zeros(x.shape, F32)
    for c in range(D_FF // ffc):
        h = jnp.dot(xb, w1_ref[:, c * ffc:(c + 1) * ffc], preferred_element_type=F32)
        h = jnp.maximum(h, 0.0)
        acc = acc + jnp.dot((h * h).astype(BF16), w2_ref[c * ffc:(c + 1) * ffc, :],
                            preferred_element_type=F32)
    o_ref[...] = _ln(ALPHA * x + acc, g_ref[...], b_ref[...])


def _mlp_layer(x2, w1_bf, w2_bf, g, b, *, tm):
    n, D = x2.shape
    row = lambda: pl.BlockSpec((1, D), lambda i: (0, 0))
    return pl.pallas_call(
        functools.partial(_mlp_kernel, ffc=1024),
        grid=(n // tm,),
        in_specs=[pl.BlockSpec((tm, D), lambda i: (i, 0)),
                  pl.BlockSpec((D, D_FF), lambda i: (0, 0), pipeline_mode=pl.Buffered(1)),
                  pl.BlockSpec((D_FF, D), lambda i: (0, 0), pipeline_mode=pl.Buffered(1)),
                  row(), row()],
        out_specs=pl.BlockSpec((tm, D), lambda i: (i, 0)),
        out_shape=jax.ShapeDtypeStruct((n, D), F32),
        compiler_params=pltpu.CompilerParams(dimension_semantics=("arbitrary",),
                                             vmem_limit_bytes=VMEM_LIMIT),
        name="mlp_ln",
    )(x2, w1_bf, w2_bf, g, b)


def _proj_kernel(x_ref, wqkv_ref, wqi_ref, wkw_ref, kng_ref, knb_ref,
                 q_ref, k_ref, v_ref, kb_ref, vb_ref, qi_ref, ki_ref, wt_ref):
    xb = x_ref[...].astype(BF16)
    hq = jnp.dot(xb, wqkv_ref[:, :D_MODEL], preferred_element_type=F32)
    q_ref[...] = (hq * Q_SCALE).astype(BF16)
    for part, (f_ref, b_ref) in enumerate(((k_ref, kb_ref), (v_ref, vb_ref)), start=1):
        h = jnp.dot(xb, wqkv_ref[:, part * D_MODEL:(part + 1) * D_MODEL], preferred_element_type=F32)
        f_ref[...] = h
        b_ref[...] = h.astype(BF16)
    hqi = jnp.dot(xb, wqi_ref[...], preferred_element_type=F32)
    for hh in range(IDX_HEADS):
        qi_ref[hh] = hqi[:, hh * IDX_DIM:(hh + 1) * IDX_DIM].astype(BF16)
    hkw = jnp.dot(xb, wkw_ref[...], preferred_element_type=F32)
    ki_ref[...] = _ln(hkw[:, :IDX_DIM], kng_ref[...], knb_ref[...])
    wt_ref[...] = hkw[:, IDX_DIM:IDX_DIM + IDX_HEADS] * (IDX_HEADS ** -0.5)


def _proj_layer(x2, wqkv_bf, wqi_bf, wkw_bf, kn_g, kn_b, *, tm):
    n, D = x2.shape
    full = lambda shp: pl.BlockSpec(shp, lambda i: tuple(0 for _ in shp))
    rows = lambda c: pl.BlockSpec((tm, c), lambda i: (i, 0))
    sds = jax.ShapeDtypeStruct
    return pl.pallas_call(
        _proj_kernel,
        grid=(n // tm,),
        in_specs=[rows(D), full((D, 3 * D_MODEL)), full((D, D_QI)), full((D, LANES)),
                  full((1, IDX_DIM)), full((1, IDX_DIM))],
        out_specs=[rows(D), rows(D), rows(D), rows(D), rows(D),
                   pl.BlockSpec((IDX_HEADS, tm, IDX_DIM), lambda i: (0, i, 0)),
                   rows(IDX_DIM), rows(IDX_HEADS)],
        out_shape=[sds((n, D), BF16), sds((n, D), F32), sds((n, D), F32), sds((n, D), BF16),
                   sds((n, D), BF16), sds((IDX_HEADS, n, IDX_DIM), BF16),
                   sds((n, IDX_DIM), F32), sds((n, IDX_HEADS), F32)],
        compiler_params=pltpu.CompilerParams(dimension_semantics=("arbitrary",),
                                             vmem_limit_bytes=VMEM_LIMIT),
        name="attn_in_proj",
    )(x2, wqkv_bf, wqi_bf, wkw_bf, kn_g, kn_b)


def _proj_t_kernel(x_ref, wqt_ref, wk_ref, wv_ref, wvt_ref, wqit_ref, wkw_ref, wkwt_ref, kng_ref, knb_ref,
                   qt_ref, k_ref, v_ref, kb_ref, vt_ref, qit_ref, ki_ref, wtt_ref):
    xb = x_ref[...].astype(BF16)
    qt_ref[...] = (_dot_nt(wqt_ref[...], xb) * Q_SCALE).astype(BF16)
    hk = jnp.dot(xb, wk_ref[...], preferred_element_type=F32)
    k_ref[...] = hk
    kb_ref[...] = hk.astype(BF16)
    v_ref[...] = jnp.dot(xb, wv_ref[...], preferred_element_type=F32)
    vt_ref[...] = _dot_nt(wvt_ref[...], xb).astype(BF16)
    qit_ref[...] = _dot_nt(wqit_ref[...], xb).astype(BF16)
    hkw = jnp.dot(xb, wkw_ref[...], preferred_element_type=F32)
    ki_ref[...] = _ln(hkw[:, :IDX_DIM], kng_ref[...], knb_ref[...])
    hkwt = _dot_nt(wkwt_ref[...], xb)
    wtt_ref[...] = hkwt[IDX_DIM:IDX_DIM + IDX_HEADS, :] * (IDX_HEADS ** -0.5)


N_PROJ_T_INPUTS = 10


def _proj_t_kernel_into(*refs, n_aliased):
    ins = refs[:N_PROJ_T_INPUTS]
    qt_ref, k_ref, v_ref, *outs = refs[N_PROJ_T_INPUTS + n_aliased:]
    _proj_t_kernel(*ins, qt_ref, k_ref.at[0], v_ref.at[0], *outs)


def _proj_t_layer(x2, w_in, kn_g, kn_b, *, tm, layer, n_layers, kv_all=None):
    n, D = x2.shape
    wq, wk, wv = (w_in[:, p * D_MODEL:(p + 1) * D_MODEL] for p in range(3))
    wqi = w_in[:, 3 * D_MODEL:3 * D_MODEL + D_QI]
    wkw = jnp.pad(w_in[:, 3 * D_MODEL + D_QI:], ((0, 0), (0, LANES - IDX_DIM - IDX_HEADS)))
    bf = lambda a: a.astype(BF16)
    weights = [bf(wq.T), bf(wk), bf(wv), bf(wv.T), bf(wqi.T), bf(wkw), bf(wkw.T)]
    full = lambda shp: pl.BlockSpec(shp, lambda i: tuple(0 for _ in shp))
    rows = lambda c: pl.BlockSpec((tm, c), lambda i: (i, 0))
    cols = lambda r: pl.BlockSpec((r, tm), lambda i: (0, i))
    slab = lambda: pl.BlockSpec((1, tm, D), lambda i: (layer, i, 0))
    sds = jax.ShapeDtypeStruct
    in_specs = [rows(D)] + [full(w.shape) for w in weights] + [full((1, IDX_DIM)), full((1, IDX_DIM))]
    operands = [x2, *weights, kn_g, kn_b]
    aliases = {}
    if kv_all is not None:
        aliases = {len(operands): 1, len(operands) + 1: 2}
        in_specs += [pl.BlockSpec(memory_space=pl.ANY), pl.BlockSpec(memory_space=pl.ANY)]
        operands += list(kv_all)
    return pl.pallas_call(
        functools.partial(_proj_t_kernel_into, n_aliased=len(aliases)),
        grid=(n // tm,),
        in_specs=in_specs,
        out_specs=[cols(D), slab(), slab(), rows(D), cols(D), cols(D_QI), rows(IDX_DIM), cols(IDX_HEADS)],
        out_shape=[sds((D, n), BF16), sds((n_layers, n, D), F32), sds((n_layers, n, D), F32), sds((n, D), BF16),
                   sds((D, n), BF16), sds((D_QI, n), BF16), sds((n, IDX_DIM), F32), sds((IDX_HEADS, n), F32)],
        input_output_aliases=aliases,
        compiler_params=pltpu.CompilerParams(dimension_semantics=("arbitrary",),
                                             vmem_limit_bytes=VMEM_LIMIT),
        name="attn_in_proj_t",
    )(*operands)


def _oproj_kernel(o_ref, x_ref, wo_ref, g_ref, b_ref, y_ref):
    h = jnp.dot(o_ref[...], wo_ref[...], preferred_element_type=F32)
    y_ref[...] = _ln(ALPHA * x_ref[...] + h, g_ref[...], b_ref[...])


def _oproj_layer(o2, x2, wo_bf, g, b, *, tm):
    n, D = x2.shape
    row = lambda: pl.BlockSpec((1, D), lambda i: (0, 0))
    return pl.pallas_call(
        _oproj_kernel,
        grid=(n // tm,),
        in_specs=[pl.BlockSpec((tm, D), lambda i: (i, 0)), pl.BlockSpec((tm, D), lambda i: (i, 0)),
                  pl.BlockSpec((D, D), lambda i: (0, 0)), row(), row()],
        out_specs=pl.BlockSpec((tm, D), lambda i: (i, 0)),
        out_shape=jax.ShapeDtypeStruct((n, D), F32),
        compiler_params=pltpu.CompilerParams(dimension_semantics=("arbitrary",)),
        name="attn_out_proj_ln",
    )(o2, x2, wo_bf, g, b)


def _search_threshold(count_ge, count_tie, rmax, rmin, ncols, ksel):
    inf = jnp.inf
    shape = rmax.shape

    def status(flo, fhi, clo):
        mid = 0.5 * flo + 0.5 * fhi
        lo_inf = flo == -inf
        hi_inf = fhi == inf
        p = mid
        p = jnp.where((flo == 0.0) & (fhi > F32_TINY), F32_TINY, p)
        p = jnp.where((flo < 0.0) & (fhi > 0.0), 0.0, p)
        p = jnp.where(lo_inf, jnp.where(fhi > rmin, rmin, -F32_MAX), p)
        p = jnp.where(hi_inf, rmax, p)
        adjacent = ~lo_inf & ~hi_inf & ((mid <= flo) | (mid >= fhi))
        done = (clo == ksel) | adjacent | (hi_inf & (flo >= rmax)) | (lo_inf & (fhi <= -F32_MAX))
        return jnp.where(done, 1.0, 0.0), p

    def search_body(st):
        it, flo, fhi, clo, chi, donef, p, _ = st
        cnt = count_ge(p)
        live = donef < 0.5
        up_lo = live & (cnt >= ksel)
        up_hi = live & (cnt < ksel)
        flo, clo = jnp.where(up_lo, p, flo), jnp.where(up_lo, cnt, clo)
        fhi, chi = jnp.where(up_hi, p, fhi), jnp.where(up_hi, cnt, chi)
        donef, p = status(flo, fhi, clo)
        return it + 1, flo, fhi, clo, chi, donef, p, (jnp.min(donef) > 0.5).astype(jnp.int32)

    flo0 = jnp.full(shape, -inf, F32)
    fhi0 = jnp.full(shape, inf, F32)
    clo0 = jnp.zeros(shape, F32) + jnp.asarray(ncols).astype(F32)
    done0, p0 = status(flo0, fhi0, clo0)
    st = lax.while_loop(lambda st: (st[0] < SEARCH_MAX_STEPS) & (st[7] == 0), search_body,
                        (jnp.int32(0), flo0, fhi0, clo0, jnp.zeros(shape, F32), done0, p0, jnp.int32(0)))
    _, thr, _, clo, chi, _, _, _ = st

    tie_rows = clo > ksel
    need = ksel - chi
    ncols_i = jnp.asarray(ncols).astype(jnp.int32)

    def tie_phase():
        def tb(_, st):
            jlo, jhi = st
            mid = (jlo + jhi) >> 1
            ok = count_tie(thr, mid) >= need
            return jnp.where(ok, jlo, mid), jnp.where(ok, mid, jhi)
        init_j = (jnp.full(shape, -1, jnp.int32), jnp.zeros(shape, jnp.int32) + (ncols_i - 1))
        _, jhi = lax.fori_loop(0, 14, tb, init_j)
        return jnp.where(tie_rows, jhi, ncols_i)

    any_tie = jnp.max(jnp.where(tie_rows, 1.0, 0.0)) > 0.0
    jcut = lax.cond(any_tie, tie_phase, lambda: jnp.zeros(shape, jnp.int32) + ncols_i)
    return thr, jcut


SUBLANES = 8
NACC = 4


def _select_bias_t(sc_ref, n_chunks, cw, tpos, ksel):
    nq = sc_ref.shape[1]
    grp = cw // SUBLANES
    inf = jnp.inf
    view = (grp // NACC, NACC, SUBLANES, nq)
    krow = ((lax.broadcasted_iota(jnp.int32, view, 0) * NACC + lax.broadcasted_iota(jnp.int32, view, 1)) * SUBLANES
            + lax.broadcasted_iota(jnp.int32, view, 2))

    def tile(v):
        return jnp.broadcast_to(v, (SUBLANES, nq))[None, None]

    def fold(tile_fn, init):
        def body(c, accs):
            rows = pl.ds(pl.multiple_of(c * cw, cw), cw)
            return tile_fn(accs, sc_ref[rows, :].reshape(view), c * cw)
        return lax.fori_loop(0, n_chunks, body, init)

    def finish(acc, op):
        return op(op(acc, axis=0), axis=0, keepdims=True)

    def count(pred):
        acc = fold(lambda a, x, base: a + jnp.sum(pred(x, base), axis=0), jnp.zeros(view[1:], F32))
        return finish(acc, jnp.sum)

    mx, mn = fold(lambda a, x, base: (jnp.maximum(a[0], jnp.max(x, axis=0)),
                                      jnp.minimum(a[1], jnp.min(jnp.where(x == -inf, inf, x), axis=0))),
                  (jnp.full(view[1:], -inf, F32), jnp.full(view[1:], inf, F32)))
    rmax = finish(mx, jnp.max)
    rmin = finish(mn, jnp.min)

    def count_ge(p):
        pb = tile(p)
        return count(lambda x, base: jnp.where(x >= pb, 1.0, 0.0))

    def count_tie(t, j):
        tb, jb = tile(t), tile(j)
        return count(lambda x, base: jnp.where(x == tb, jnp.where(krow + base <= jb, 1.0, 0.0), 0.0))

    thr, jcut = _search_threshold(count_ge, count_tie, rmax, rmin, n_chunks * cw, ksel)
    tb, jb = tile(thr), tile(jnp.minimum(jcut, tpos))

    def write(c, carry):
        rows = pl.ds(pl.multiple_of(c * cw, cw), cw)
        x = sc_ref[rows, :].reshape(view)
        tie_sel = jnp.where(krow + c * cw <= jb, 0.0, NEG_BIAS)
        bias = jnp.where(x > tb, 0.0, jnp.where(x == tb, tie_sel, NEG_BIAS))
        sc_ref[rows, :] = bias.reshape(cw, nq)
        return carry
    lax.fori_loop(0, n_chunks, write, 0)


def _select_bias(sc_ref, r0, nr, n_chunks, cw, tpos, ksel):
    sub = min(nr, SEL_SUB)
    nsub = nr // sub
    ncols = n_chunks * cw
    nt = cw // LANES
    lane = lax.broadcasted_iota(jnp.int32, (sub, LANES), 1)
    inf = jnp.inf

    def fold(tile_fn, init, cols=(), store=False):
        res = []
        for r in range(nsub):
            rows = slice(r0 + r * sub, r0 + (r + 1) * sub)
            bc = [jnp.broadcast_to(c[:, r * sub:(r + 1) * sub], (LANES, sub)).T for c in cols]

            def body(c, accs, rows=rows, bc=bc):
                cs = pl.ds(pl.multiple_of(c * cw, cw), cw)
                x = sc_ref[rows, cs]
                outs = []
                for j in range(nt):
                    accs = tile_fn(accs, x[:, j * LANES:(j + 1) * LANES], c * cw + j * LANES, bc)
                    if store:
                        outs.append(accs)
                if store:
                    sc_ref[rows, cs] = jnp.concatenate(outs, axis=1) if nt > 1 else outs[0]
                    return 0
                return accs
            res.append(lax.fori_loop(0, n_chunks, body, init))
        return res

    def row_reduce(parts, k, op):
        return jnp.concatenate([op(p[k].T, axis=0, keepdims=True) for p in parts], axis=1)

    def count(pred, cols):
        parts = fold(lambda a, x, base, bc: (a[0] + pred(x, base, bc),), (jnp.zeros((sub, LANES), F32),), cols)
        return row_reduce(parts, 0, jnp.sum)

    parts = fold(lambda a, x, base, bc: (jnp.maximum(a[0], x), jnp.minimum(a[1], jnp.where(x == -inf, inf, x))),
                 (jnp.full((sub, LANES), -inf, F32), jnp.full((sub, LANES), inf, F32)))
    rmax = row_reduce(parts, 0, jnp.max)
    rmin = row_reduce(parts, 1, jnp.min)

    thr, jcut = _search_threshold(
        lambda p: count(lambda x, base, bc: jnp.where(x >= bc[0], 1.0, 0.0), (p,)),
        lambda t, j: count(lambda x, base, bc: jnp.where(x == bc[0], jnp.where(lane + base <= bc[1], 1.0, 0.0), 0.0),
                           (t, j)),
        rmax, rmin, ncols, ksel)

    def bias_tile(_, x, base, bc):
        tie_sel = jnp.where(lane + base <= bc[1], 0.0, NEG_BIAS)
        return jnp.where(x > bc[0], 0.0, jnp.where(x == bc[0], tie_sel, NEG_BIAS))
    fold(bias_tile, 0, (thr, jnp.minimum(jcut, tpos)), store=True)


SEL_ROWS = 256


def _attn_prompt_kernel(qmap_ref, kmap_ref, q_ref, qi_ref, wt_ref, ki_ref, k_ref, v_ref, o_ref,
                        sc_ref, wb_ref, m_ref, l_ref, a_ref, acc_ref, s_ref, p_ref, *, tq, tk, ksel):
    n = pl.program_id(1)
    i = qmap_ref[n]
    kj = kmap_ref[n]
    last = ((i + 1) * tq - 1) // tk

    @pl.when(kj == 0)
    def _():
        wts = wt_ref[0] * (IDX_DIM ** -0.5)
        for h in range(IDX_HEADS):
            wb_ref[h] = jnp.broadcast_to(wts[:, h:h + 1], (tq, LANES))
        tpos = i * tq + lax.broadcasted_iota(jnp.int32, (tq, 1), 0)

        def chunk_scores(c, masked):
            cols = pl.ds(pl.multiple_of(c * tk, tk), tk)
            kic = ki_ref[0, cols, :].astype(BF16)
            sc = None
            for h in range(IDX_HEADS):
                d = jnp.maximum(_dot_nt(qi_ref[h, 0], kic), 0.0)
                d = d * jnp.concatenate([wb_ref[h]] * (tk // LANES), axis=1)
                sc = d if sc is None else sc + d
            if masked:
                kidx = c * tk + lax.broadcasted_iota(jnp.int32, (tq, tk), 1)
                sc = jnp.where(kidx <= tpos, sc, -jnp.inf)
            sc_ref[:, cols] = sc

        def body(c, carry):
            chunk_scores(c, False)
            return carry
        lax.fori_loop(0, last, body, 0)
        chunk_scores(last, True)

        for r in range(tq // SEL_ROWS):
            tpos_row = i * tq + r * SEL_ROWS + lax.broadcasted_iota(jnp.int32, (1, SEL_ROWS), 1)
            _select_bias(sc_ref, r * SEL_ROWS, SEL_ROWS, last + 1, tk, tpos_row, ksel)
        m_ref[...] = jnp.full(m_ref.shape, NEG_BIAS, F32)
        l_ref[...] = jnp.zeros(l_ref.shape, F32)
        acc_ref[...] = jnp.zeros(acc_ref.shape, F32)

    cols = pl.ds(pl.multiple_of(kj * tk, tk), tk)
    heads = [slice(h * HEAD_DIM, (h + 1) * HEAD_DIM) for h in range(N_HEADS)]
    rep = tk // LANES
    for h, hs in enumerate(heads):
        s = _dot_nt(q_ref[0, :, hs], k_ref[0, :, hs]) + sc_ref[:, cols]
        s_ref[h] = s
        m_old = m_ref[h]
        m_new = jnp.maximum(m_old, jnp.broadcast_to(jnp.max(s, axis=1, keepdims=True), (tq, LANES)))
        a_ref[h] = jnp.exp2(m_old - m_new)
        m_ref[h] = m_new
    for h, hs in enumerate(heads):
        p = jnp.exp2(s_ref[h] - jnp.concatenate([m_ref[h]] * rep, axis=1))
        l_ref[h] = a_ref[h][:, :1] * l_ref[h] + jnp.sum(p, axis=1, keepdims=True)
        p_ref[h] = p.astype(BF16)
    for h, hs in enumerate(heads):
        acc_ref[:, hs] = a_ref[h] * acc_ref[:, hs] + jnp.dot(p_ref[h], v_ref[0, :, hs],
                                                             preferred_element_type=F32)

    @pl.when(kj == last)
    def _():
        for h in range(N_HEADS):
            hs = slice(h * HEAD_DIM, (h + 1) * HEAD_DIM)
            o_ref[0, :, hs] = (acc_ref[:, hs] / l_ref[h]).astype(BF16)


def _attn_prompt(q, qi, wt, ki, kb, vb, *, tq=256, tk=512):
    B, T, D = q.shape
    ksel = min(TOPK_MAX, T // 4)
    nq = T // tq
    qmap, kmap = [], []
    for i in range(nq):
        for kj in range(((i + 1) * tq - 1) // tk + 1):
            qmap.append(i)
            kmap.append(kj)
    qmap = jnp.asarray(np.asarray(qmap, np.int32))
    kmap = jnp.asarray(np.asarray(kmap, np.int32))
    grid_spec = pltpu.PrefetchScalarGridSpec(
        num_scalar_prefetch=2,
        grid=(B, int(qmap.shape[0])),
        in_specs=[pl.BlockSpec((1, tq, D), lambda b, n, qm, km: (b, qm[n], 0)),
                  pl.BlockSpec((IDX_HEADS, 1, tq, IDX_DIM), lambda b, n, qm, km: (0, b, qm[n], 0)),
                  pl.BlockSpec((1, tq, IDX_HEADS), lambda b, n, qm, km: (b, qm[n], 0)),
                  pl.BlockSpec((1, T, IDX_DIM), lambda b, n, qm, km: (b, 0, 0)),
                  pl.BlockSpec((1, tk, D), lambda b, n, qm, km: (b, km[n], 0)),
                  pl.BlockSpec((1, tk, D), lambda b, n, qm, km: (b, km[n], 0))],
        out_specs=pl.BlockSpec((1, tq, D), lambda b, n, qm, km: (b, qm[n], 0)),
        scratch_shapes=[pltpu.VMEM((tq, T), F32),
                        pltpu.VMEM((IDX_HEADS, tq, LANES), F32),
                        pltpu.VMEM((N_HEADS, tq, LANES), F32),
                        pltpu.VMEM((N_HEADS, tq, 1), F32),
                        pltpu.VMEM((N_HEADS, tq, LANES), F32),
                        pltpu.VMEM((tq, D), F32),
                        pltpu.VMEM((N_HEADS, tq, tk), F32),
                        pltpu.VMEM((N_HEADS, tq, tk), BF16)])
    return pl.pallas_call(
        functools.partial(_attn_prompt_kernel, tq=tq, tk=tk, ksel=ksel),
        grid_spec=grid_spec,
        out_shape=jax.ShapeDtypeStruct((B, T, D), BF16),
        compiler_params=pltpu.CompilerParams(dimension_semantics=("arbitrary", "arbitrary"),
                                             vmem_limit_bytes=VMEM_LIMIT),
        name="attn_prompt",
    )(qmap, kmap, q, qi, wt, ki, kb, vb)


def _attn_prompt_t_kernel(qmap_ref, kmap_ref, qt_ref, qit_ref, wtt_ref, ki_ref, k_ref, vt_ref, o_ref,
                          sc_ref, wb_ref, m_ref, l_ref, a_ref, acc_ref, s_ref, p_ref, *, tq, tk, ksel):
    n = pl.program_id(1)
    i = qmap_ref[n]
    kj = kmap_ref[n]
    last = ((i + 1) * tq - 1) // tk
    grp = tk // SUBLANES

    def keyred(x, op):
        return op(op(x.reshape(grp // NACC, NACC, SUBLANES, tq), axis=0), axis=0)

    def rep(v8, rows):
        return jnp.concatenate([v8] * (rows // SUBLANES), axis=0)

    @pl.when(kj == 0)
    def _():
        wts = wtt_ref[...] * (IDX_DIM ** -0.5)
        for h in range(IDX_HEADS):
            wb_ref[h] = jnp.broadcast_to(wts[h:h + 1, :], (SUBLANES, tq))
        tpos = i * tq + lax.broadcasted_iota(jnp.int32, (1, tq), 1)

        def chunk_scores(c, masked):
            rows = pl.ds(pl.multiple_of(c * tk, tk), tk)
            kic = ki_ref[rows, :].astype(BF16)
            sc = None
            for h in range(IDX_HEADS):
                d = jnp.dot(kic, qit_ref[h * IDX_DIM:(h + 1) * IDX_DIM, :], preferred_element_type=F32)
                d = jnp.maximum(d, 0.0) * rep(wb_ref[h], tk)
                sc = d if sc is None else sc + d
            if masked:
                kidx = c * tk + lax.broadcasted_iota(jnp.int32, (tk, tq), 0)
                sc = jnp.where(kidx <= tpos, sc, -jnp.inf)
            sc_ref[rows, :] = sc

        def body(c, carry):
            chunk_scores(c, False)
            return carry
        lax.fori_loop(0, last, body, 0)
        chunk_scores(last, True)

        _select_bias_t(sc_ref, last + 1, tk, tpos, ksel)
        m_ref[...] = jnp.full(m_ref.shape, NEG_BIAS, F32)
        l_ref[...] = jnp.zeros(l_ref.shape, F32)
        acc_ref[...] = jnp.zeros(acc_ref.shape, F32)

    rows = pl.ds(pl.multiple_of(kj * tk, tk), tk)
    heads = [slice(h * HEAD_DIM, (h + 1) * HEAD_DIM) for h in range(N_HEADS)]
    for h, hs in enumerate(heads):
        s_ref[h] = jnp.dot(k_ref[:, hs], qt_ref[hs, :], preferred_element_type=F32) + sc_ref[rows, :]
    for h, hs in enumerate(heads):
        m_old = m_ref[h]
        smax = jnp.max(keyred(s_ref[h], jnp.max), axis=0, keepdims=True)
        m_new = jnp.maximum(m_old, jnp.broadcast_to(smax, (SUBLANES, tq)))
        a_ref[h] = jnp.exp2(m_old - m_new)
        m_ref[h] = m_new
    for h, hs in enumerate(heads):
        p = jnp.exp2(s_ref[h] - rep(m_ref[h], tk))
        l_ref[h] = a_ref[h] * l_ref[h] + keyred(p, jnp.sum)
        p_ref[h] = p.astype(BF16)
    for h, hs in enumerate(heads):
        acc_ref[h] = rep(a_ref[h], HEAD_DIM) * acc_ref[h] + jnp.dot(vt_ref[hs, :], p_ref[h],
                                                                     preferred_element_type=F32)

    @pl.when(kj == last)
    def _():
        for h, hs in enumerate(heads):
            l = jnp.broadcast_to(jnp.sum(l_ref[h], axis=0, keepdims=True), (SUBLANES, tq))
            o_ref[:, hs] = (acc_ref[h] / rep(l, HEAD_DIM)).T.astype(BF16)


def _attn_prompt_t(B, qt, qit, wtt, ki, kb, vt, *, tq=512, tk=512):
    n, D = kb.shape
    T = n // B
    ksel = min(TOPK_MAX, T // 4)
    nq, nk = T // tq, T // tk
    qmap, kmap = [], []
    for i in range(nq):
        for kj in range(((i + 1) * tq - 1) // tk + 1):
            qmap.append(i)
            kmap.append(kj)
    qmap = jnp.asarray(np.asarray(qmap, np.int32))
    kmap = jnp.asarray(np.asarray(kmap, np.int32))
    qcol = lambda r: pl.BlockSpec((r, tq), lambda b, s, qm, km: (0, b * nq + qm[s]))
    grid_spec = pltpu.PrefetchScalarGridSpec(
        num_scalar_prefetch=2,
        grid=(B, int(qmap.shape[0])),
        in_specs=[qcol(D), qcol(D_QI), qcol(IDX_HEADS),
                  pl.BlockSpec((T, IDX_DIM), lambda b, s, qm, km: (b, 0)),
                  pl.BlockSpec((tk, D), lambda b, s, qm, km: (b * nk + km[s], 0)),
                  pl.BlockSpec((D, tk), lambda b, s, qm, km: (0, b * nk + km[s]))],
        out_specs=pl.BlockSpec((tq, D), lambda b, s, qm, km: (b * nq + qm[s], 0)),
        scratch_shapes=[pltpu.VMEM((T, tq), F32),
                        pltpu.VMEM((IDX_HEADS, SUBLANES, tq), F32),
                        pltpu.VMEM((N_HEADS, SUBLANES, tq), F32),
                        pltpu.VMEM((N_HEADS, SUBLANES, tq), F32),
                        pltpu.VMEM((N_HEADS, SUBLANES, tq), F32),
                        pltpu.VMEM((N_HEADS, HEAD_DIM, tq), F32),
                        pltpu.VMEM((N_HEADS, tk, tq), F32),
                        pltpu.VMEM((N_HEADS, tk, tq), BF16)])
    return pl.pallas_call(
        functools.partial(_attn_prompt_t_kernel, tq=tq, tk=tk, ksel=ksel),
        grid_spec=grid_spec,
        out_shape=jax.ShapeDtypeStruct((n, D), BF16),
        compiler_params=pltpu.CompilerParams(dimension_semantics=("arbitrary", "arbitrary"),
                                             vmem_limit_bytes=VMEM_LIMIT),
        name="attn_prompt_t",
    )(qmap, kmap, qt, qit, wtt, ki, kb, vt)


KI_PAGES_PER_STEP = 16
KV_PAGES_PER_STEP = 8
SEL_CHUNK = 5 * LANES


def _sample_scores_kernel(pt_ref, qi_ref, wt_ref, kin_ref, *rest, n_pages, tq):
    del pt_ref
    g1 = KI_PAGES_PER_STEP
    kidx_refs, sc_ref = rest[:g1], rest[g1]
    s = pl.program_id(1)
    wcol = jnp.broadcast_to(wt_ref[0] * (IDX_DIM ** -0.5), (IDX_HEADS * tq, PAGE_SIZE))

    def chunk_scores(kit_chunk):
        d = jnp.maximum(jnp.dot(qi_ref[0], kit_chunk.astype(BF16), preferred_element_type=F32), 0.0) * wcol
        sc = d[0:tq]
        for h in range(1, IDX_HEADS):
            sc = sc + d[h * tq:(h + 1) * tq]
        return sc

    for g in range(g1):
        page = s * g1 + g
        sc_ref[0, :, pl.ds(pl.multiple_of(page * PAGE_SIZE, PAGE_SIZE), PAGE_SIZE)] = chunk_scores(kidx_refs[g][0, 0])

    @pl.when(s == n_pages // g1 - 1)
    def _():
        qrow = lax.broadcasted_iota(jnp.int32, (tq, PAGE_SIZE), 0)
        jcol = lax.broadcasted_iota(jnp.int32, (tq, PAGE_SIZE), 1)
        sc_ref[0, :, n_pages * PAGE_SIZE:] = jnp.where(jcol <= qrow, chunk_scores(kin_ref[0]), -jnp.inf)


def _sample_scores(layer, page_table, qi, wt, ki_new, cache_kidx, *, tq):
    Bd, n_pages = page_table.shape
    g1 = KI_PAGES_PER_STEP
    ncols = (n_pages + 1) * PAGE_SIZE

    def per_b(shape):
        return pl.BlockSpec((1,) + shape, lambda b, s, pt: (b,) + tuple(0 for _ in shape))

    def kidx_spec(g):
        return pl.BlockSpec((1, 1, IDX_DIM, PAGE_SIZE), lambda b, s, pt: (layer, pt[b, s * g1 + g], 0, 0))

    grid_spec = pltpu.PrefetchScalarGridSpec(
        num_scalar_prefetch=1,
        grid=(Bd, n_pages // g1),
        in_specs=[per_b((IDX_HEADS * tq, IDX_DIM)), per_b((IDX_HEADS * tq, 1)), per_b((IDX_DIM, PAGE_SIZE))]
                 + [kidx_spec(g) for g in range(g1)],
        out_specs=per_b((tq, ncols)))
    return pl.pallas_call(
        functools.partial(_sample_scores_kernel, n_pages=n_pages, tq=tq),
        grid_spec=grid_spec,
        out_shape=jax.ShapeDtypeStruct((Bd, tq, ncols), F32),
        compiler_params=pltpu.CompilerParams(dimension_semantics=("arbitrary", "arbitrary")),
        name="sample_scores",
    )(page_table, qi, wt, ki_new, *([cache_kidx] * g1))


def _sample_select_kernel(sc_ref, o_ref, *, tq, past, ksel):
    o_ref[...] = sc_ref[...]
    nr, ncols = o_ref.shape
    tpos = past + lax.rem(lax.broadcasted_iota(jnp.int32, (1, nr), 1), tq)
    _select_bias(o_ref, 0, nr, ncols // SEL_CHUNK, SEL_CHUNK, tpos, ksel)


def _sample_select(sc2, *, tq, past, ksel):
    n, ncols = sc2.shape
    nr = min(SEL_ROWS, n)
    return pl.pallas_call(
        functools.partial(_sample_select_kernel, tq=tq, past=past, ksel=ksel),
        grid=(n // nr,),
        in_specs=[pl.BlockSpec((nr, ncols), lambda i: (i, 0))],
        out_specs=pl.BlockSpec((nr, ncols), lambda i: (i, 0)),
        out_shape=jax.ShapeDtypeStruct((n, ncols), F32),
        compiler_params=pltpu.CompilerParams(dimension_semantics=("arbitrary",), vmem_limit_bytes=VMEM_LIMIT),
        name="sample_select",
    )(sc2)


def _sample_attend_kernel(pt_ref, q_ref, bias_ref, kn_ref, vn_ref, *rest, n_pages, tq):
    del pt_ref
    g2 = KV_PAGES_PER_STEP
    kpage_refs, vpage_refs = rest[:g2], rest[g2:2 * g2]
    o_ref, s_ref, l_ref, acc_ref = rest[2 * g2:]
    nsk = n_pages // g2
    s = pl.program_id(1)
    new_cols = slice(n_pages * PAGE_SIZE, (n_pages + 1) * PAGE_SIZE)

    def head_rows(h):
        return slice(h * tq, (h + 1) * tq)

    def head_cols(h):
        return slice(h * HEAD_DIM, (h + 1) * HEAD_DIM)

    @pl.when(s < nsk)
    def _():
        for g in range(g2):
            cols = pl.ds(pl.multiple_of((s * g2 + g) * PAGE_SIZE, PAGE_SIZE), PAGE_SIZE)
            bias = bias_ref[0, :, cols]
            for h in range(N_HEADS):
                kh = kpage_refs[g][0, 0, pl.ds(h, PAGE_SIZE, stride=N_HEADS), :].astype(BF16)
                s_ref[head_rows(h), cols] = _dot_nt(q_ref[0, :, head_cols(h)], kh) + bias

    @pl.when(s == nsk - 1)
    def _():
        bias = bias_ref[0, :, new_cols]
        for h in range(N_HEADS):
            s_ref[head_rows(h), new_cols] = _dot_nt(q_ref[0, :, head_cols(h)], kn_ref[0, :, head_cols(h)]) + bias
        sall = s_ref[...]
        p = jnp.exp2(sall - jnp.max(sall, axis=1, keepdims=True))
        l_ref[...] = jnp.sum(p, axis=1, keepdims=True)
        s_ref[...] = p
        acc_ref[...] = jnp.zeros(acc_ref.shape, F32)

    @pl.when(s >= nsk)
    def _():
        for h in range(N_HEADS):
            o = jnp.zeros((tq, HEAD_DIM), F32)
            for g in range(g2):
                cols = pl.ds(pl.multiple_of(((s - nsk) * g2 + g) * PAGE_SIZE, PAGE_SIZE), PAGE_SIZE)
                vh = vpage_refs[g][0, 0, pl.ds(h, PAGE_SIZE, stride=N_HEADS), :].astype(BF16)
                o = o + jnp.dot(s_ref[head_rows(h), cols].astype(BF16), vh, preferred_element_type=F32)
            acc_ref[:, head_cols(h)] += o

    @pl.when(s == 2 * nsk - 1)
    def _():
        for h in range(N_HEADS):
            o = acc_ref[:, head_cols(h)] + jnp.dot(s_ref[head_rows(h), new_cols].astype(BF16),
                                                   vn_ref[0, :, head_cols(h)], preferred_element_type=F32)
            o_ref[0, :, head_cols(h)] = (o / l_ref[head_rows(h), :]).astype(BF16)


def _sample_attend(layer, page_table, q, bias, kb_new, vb_new, cache_k, cache_v):
    Bd, tq, D = q.shape
    n_pages = page_table.shape[1]
    ncols = (n_pages + 1) * PAGE_SIZE
    g2 = KV_PAGES_PER_STEP
    nsk = n_pages // g2

    def per_b(shape):
        return pl.BlockSpec((1,) + shape, lambda b, s, pt: (b,) + tuple(0 for _ in shape))

    def page_spec(g, first_step):
        return pl.BlockSpec((1, 1, PAGE_SIZE * N_HEADS, HEAD_DIM),
                            lambda b, s, pt: (layer, pt[b, jnp.clip(s - first_step, 0, nsk - 1) * g2 + g], 0, 0))

    grid_spec = pltpu.PrefetchScalarGridSpec(
        num_scalar_prefetch=1,
        grid=(Bd, 2 * nsk),
        in_specs=[per_b((tq, D)), per_b((tq, ncols)), per_b((PAGE_SIZE, D)), per_b((PAGE_SIZE, D))]
                 + [page_spec(g, 0) for g in range(g2)] + [page_spec(g, nsk) for g in range(g2)],
        out_specs=per_b((tq, D)),
        scratch_shapes=[pltpu.VMEM((N_HEADS * tq, ncols), F32),
                        pltpu.VMEM((N_HEADS * tq, 1), F32),
                        pltpu.VMEM((tq, D), F32)])
    return pl.pallas_call(
        functools.partial(_sample_attend_kernel, n_pages=n_pages, tq=tq),
        grid_spec=grid_spec,
        out_shape=jax.ShapeDtypeStruct((Bd, tq, D), BF16),
        compiler_params=pltpu.CompilerParams(dimension_semantics=("arbitrary", "arbitrary"),
                                             vmem_limit_bytes=VMEM_LIMIT),
        name="sample_attend",
    )(page_table, q, bias, kb_new, vb_new, *([cache_k] * g2), *([cache_v] * g2))


def _pad_rows(a, n):
    return jnp.pad(a, ((0, 0), (0, n - a.shape[1]), (0, 0)))


def kernel(x_prompt, x_sample, state_pool, cache_k, cache_v, cache_kidx, page_table, pool_w, pool_scale,
           attn_w_in, attn_kn_g, attn_kn_b, attn_w_o, mlp_w1, mlp_w2, ln_g, ln_b):
    B, T, D = x_prompt.shape
    Bd, Td, _ = x_sample.shape
    n_attn, n_phys = cache_k.shape[0], cache_k.shape[1]
    n_pages = page_table.shape[1]
    past = n_pages * PAGE_SIZE
    xp, xs = x_prompt, x_sample
    ck = cache_k.reshape(n_attn, n_phys, PAGE_SIZE * N_HEADS, HEAD_DIM)
    cv = cache_v.reshape(n_attn, n_phys, PAGE_SIZE * N_HEADS, HEAD_DIM)
    ckit = cache_kidx.transpose(0, 1, 3, 2)
    pool_p, pool_s = [], []
    kip, ksm, vsm, kism = [], [], [], []
    kv_prompt = None
    for i in range(DEPTH):
        j = i // 2
        g0, b0 = ln_g[i, 0][None], ln_b[i, 0][None]
        g1, b1 = ln_g[i, 1][None], ln_b[i, 1][None]
        if i % 2 == 0:
            w_bf = pool_w[j].astype(BF16)
            scale = pool_scale[j][None]
            pool_p.append(xp[:, T - POOL_BUF:])
            halo_s = jnp.concatenate([jnp.zeros((Bd, 1, D), xs.dtype), state_pool[j].astype(xs.dtype)], axis=1)
            pool_s.append(jnp.concatenate([halo_s, xs], axis=1)[:, -POOL_BUF:])
            xp = _pool_layer(xp, xp, w_bf, scale, g0, b0, tq=512, start=0, first_is_zero=True)
            xs = _pool_layer(xs, halo_s, w_bf, scale, g0, b0, tq=Td, start=past, first_is_zero=False)
        else:
            w_in = attn_w_in[j]
            wqkv = w_in[:, :3 * D_MODEL].astype(BF16)
            wqi = w_in[:, 3 * D_MODEL:3 * D_MODEL + D_QI].astype(BF16)
            wkw = jnp.pad(w_in[:, 3 * D_MODEL + D_QI:], ((0, 0), (0, LANES - IDX_DIM - IDX_HEADS))).astype(BF16)
            kng, knb = attn_kn_g[j][None], attn_kn_b[j][None]
            wo = attn_w_o[j].astype(BF16)

            qt, k_all, v_all, kb, vt, qit, ki, wtt = _proj_t_layer(
                xp.reshape(B * T, D), w_in, kng, knb, tm=512, layer=j, n_layers=n_attn, kv_all=kv_prompt)
            kv_prompt = (k_all, v_all)
            o = _attn_prompt_t(B, qt, qit, wtt, ki, kb, vt)
            xp = _oproj_layer(o, xp.reshape(B * T, D), wo, g0, b0, tm=512).reshape(B, T, D)
            kip.append(ki.reshape(B, T, IDX_DIM))

            n = Bd * Td
            q, k, v, kb, vb, qi, ki, wt = _proj_layer(xs.reshape(n, D), wqkv, wqi, wkw, kng, knb, tm=n)
            qi_s = qi.reshape(IDX_HEADS, Bd, Td, IDX_DIM).transpose(1, 0, 2, 3).reshape(Bd, IDX_HEADS * Td, IDX_DIM)
            wt_s = wt.reshape(Bd, Td, IDX_HEADS).transpose(0, 2, 1).reshape(Bd, IDX_HEADS * Td, 1)
            kit_new = _pad_rows(ki.reshape(Bd, Td, IDX_DIM), PAGE_SIZE).transpose(0, 2, 1)
            sc = _sample_scores(j, page_table, qi_s, wt_s, kit_new, ckit, tq=Td)
            bias = _sample_select(sc.reshape(n, sc.shape[-1]), tq=Td, past=past,
                                  ksel=min(TOPK_MAX, (past + Td) // 4))
            o = _sample_attend(j, page_table, q.reshape(Bd, Td, D), bias.reshape(Bd, Td, -1),
                               _pad_rows(kb.reshape(Bd, Td, D), PAGE_SIZE),
                               _pad_rows(vb.reshape(Bd, Td, D), PAGE_SIZE), ck, cv)
            xs = _oproj_layer(o.reshape(n, D), xs.reshape(n, D), wo, g0, b0, tm=n).reshape(Bd, Td, D)
            ksm.append(k.reshape(Bd, Td, N_HEADS, HEAD_DIM))
            vsm.append(v.reshape(Bd, Td, N_HEADS, HEAD_DIM))
            kism.append(ki.reshape(Bd, Td, IDX_DIM))
        w1, w2 = mlp_w1[i].astype(BF16), mlp_w2[i].astype(BF16)
        xp = _mlp_layer(xp.reshape(B * T, D), w1, w2, g1, b1, tm=512).reshape(B, T, D)
        xs = _mlp_layer(xs.reshape(Bd * Td, D), w1, w2, g1, b1, tm=Bd * Td).reshape(Bd, Td, D)
    k_prompt, v_prompt = (a.reshape(n_attn, B, T, N_HEADS, HEAD_DIM) for a in kv_prompt)
    return (xp, xs, jnp.stack(pool_p), jnp.stack(pool_s), k_prompt, v_prompt, jnp.stack(kip),
            jnp.stack(ksm), jnp.stack(vsm), jnp.stack(kism))
```

```python
import functools
import math

import numpy as np
import jax
import jax.numpy as jnp
from jax import lax
from jax.experimental import pallas as pl
from jax.experimental.pallas import tpu as pltpu

D_MODEL = 1024
DEPTH = 4
PAST_LEN = 8192
PAGE_SIZE = 128
POOL_WINDOWS = (2, 4, 8, 16)
POOL_GROUP = D_MODEL // len(POOL_WINDOWS)
POOL_BUF = max(POOL_WINDOWS) - 1
HALO = POOL_BUF + 1
N_HEADS = 8
HEAD_DIM = D_MODEL // N_HEADS
IDX_HEADS = 8
IDX_DIM = 64
TOPK_MAX = 256
D_FF = 4 * D_MODEL
ALPHA = (2 * DEPTH) ** 0.25
LN_EPS = 1e-5
D_QI = IDX_HEADS * IDX_DIM
Q_SCALE = HEAD_DIM ** -0.5 * math.log2(math.e)

LANES = 128
NEG_BIAS = -1e30
F32_MAX = float(np.finfo(np.float32).max)
F32_TINY = float(np.finfo(np.float32).tiny)
VMEM_LIMIT = 52 * 1024 * 1024
SEARCH_MAX_STEPS = 400
SEL_SUB = 128

F32 = jnp.float32
BF16 = jnp.bfloat16


def _ln(y, g, b):
    mu = jnp.mean(y, axis=-1, keepdims=True)
    yc = y - mu
    var = jnp.mean(yc * yc, axis=-1, keepdims=True)
    return yc * lax.rsqrt(var + LN_EPS) * g + b


def _dot_nt(a, b):
    return lax.dot_general(a, b, (((1,), (1,)), ((), ())), preferred_element_type=F32)


def _pool_kernel(x_ref, halo_ref, w_ref, sc_ref, g_ref, b_ref, o_ref, *, tq, start, first_is_zero):
    i = pl.program_id(1)
    x = x_ref[0]
    halo = halo_ref[0]
    if first_is_zero:
        halo = jnp.where(i == 0, 0.0, halo)
    xa = jnp.concatenate([halo, x], axis=0)
    pos = start + i * tq + lax.broadcasted_iota(jnp.int32, (tq, 1), 0)
    outs = []
    for g, w in enumerate(POOL_WINDOWS):
        sl = slice(g * POOL_GROUP, (g + 1) * POOL_GROUP)
        s = xa[:, sl]
        sh = 1
        while sh < w:
            s = s + pltpu.roll(s, sh, axis=0)
            sh *= 2
        cnt = jnp.minimum(w, pos + 1).astype(F32)
        p = s[HALO:] / cnt - x[:, sl]
        outs.append(jnp.dot(p.astype(BF16), w_ref[g], preferred_element_type=F32))
    y = jnp.concatenate(outs, axis=1) * sc_ref[...]
    o_ref[0] = _ln(ALPHA * x + y, g_ref[...], b_ref[...])


def _pool_layer(x, halo, w_bf, scale, g, b, *, tq, start, first_is_zero):
    B, T, D = x.shape
    nq = T // tq
    if first_is_zero:
        halo_spec = pl.BlockSpec((1, HALO, D), lambda bb, i: (bb, jnp.maximum(i * (tq // HALO) - 1, 0), 0))
    else:
        halo_spec = pl.BlockSpec((1, HALO, D), lambda bb, i: (bb, 0, 0))
    row = lambda: pl.BlockSpec((1, D), lambda bb, i: (0, 0))
    return pl.pallas_call(
        functools.partial(_pool_kernel, tq=tq, start=start, first_is_zero=first_is_zero),
        grid=(B, nq),
        in_specs=[pl.BlockSpec((1, tq, D), lambda bb, i: (bb, i, 0)),
                  halo_spec,
                  pl.BlockSpec((len(POOL_WINDOWS), POOL_GROUP, POOL_GROUP), lambda bb, i: (0, 0, 0)),
                  row(), row(), row()],
        out_specs=pl.BlockSpec((1, tq, D), lambda bb, i: (bb, i, 0)),
        out_shape=jax.ShapeDtypeStruct((B, T, D), F32),
        compiler_params=pltpu.CompilerParams(dimension_semantics=("arbitrary", "arbitrary")),
        name="pool_mix_ln",
    )(x, halo, w_bf, scale, g, b)


def _mlp_kernel(x_ref, w1_ref, w2_ref, g_ref, b_ref, o_ref, *, ffc):
    x = x_ref[...]
    xb = x.astype(BF16)
    acc = jnp.zeros(x.shape, F32)
    for c in range(D_FF // ffc):
        h = jnp.dot(xb, w1_ref[:, c * ffc:(c + 1) * ffc], preferred_element_type=F32)
        h = jnp.maximum(h, 0.0)
        acc = acc + jnp.dot((h * h).astype(BF16), w2_ref[c * ffc:(c + 1) * ffc, :],
                            preferred_element_type=F32)
    o_ref[...] = _ln(ALPHA * x + acc, g_ref[...], b_ref[...])


def _mlp_layer(x2, w1_bf, w2_bf, g, b, *, tm):
    n, D = x2.shape
    row = lambda: pl.BlockSpec((1, D), lambda i: (0, 0))
    return pl.pallas_call(
        functools.partial(_mlp_kernel, ffc=1024),
        grid=(n // tm,),
        in_specs=[pl.BlockSpec((tm, D), lambda i: (i, 0)),
                  pl.BlockSpec((D, D_FF), lambda i: (0, 0), pipeline_mode=pl.Buffered(1)),
                  pl.BlockSpec((D_FF, D), lambda i: (0, 0), pipeline_mode=pl.Buffered(1)),
                  row(), row()],
        out_specs=pl.BlockSpec((tm, D), lambda i: (i, 0)),
        out_shape=jax.ShapeDtypeStruct((n, D), F32),
        compiler_params=pltpu.CompilerParams(dimension_semantics=("arbitrary",),
                                             vmem_limit_bytes=VMEM_LIMIT),
        name="mlp_ln",
    )(x2, w1_bf, w2_bf, g, b)


def _proj_kernel(x_ref, wqkv_ref, wqi_ref, wkw_ref, kng_ref, knb_ref,
                 q_ref, k_ref, v_ref, kb_ref, vb_ref, qi_ref, ki_ref, wt_ref):
    xb = x_ref[...].astype(BF16)
    hq = jnp.dot(xb, wqkv_ref[:, :D_MODEL], preferred_element_type=F32)
    q_ref[...] = (hq * Q_SCALE).astype(BF16)
    for part, (f_ref, b_ref) in enumerate(((k_ref, kb_ref), (v_ref, vb_ref)), start=1):
        h = jnp.dot(xb, wqkv_ref[:, part * D_MODEL:(part + 1) * D_MODEL], preferred_element_type=F32)
        f_ref[...] = h
        b_ref[...] = h.astype(BF16)
    hqi = jnp.dot(xb, wqi_ref[...], preferred_element_type=F32)
    for hh in range(IDX_HEADS):
        qi_ref[hh] = hqi[:, hh * IDX_DIM:(hh + 1) * IDX_DIM].astype(BF16)
    hkw = jnp.dot(xb, wkw_ref[...], preferred_element_type=F32)
    ki_ref[...] = _ln(hkw[:, :IDX_DIM], kng_ref[...], knb_ref[...])
    wt_ref[...] = hkw[:, IDX_DIM:IDX_DIM + IDX_HEADS] * (IDX_HEADS ** -0.5)


def _proj_layer(x2, wqkv_bf, wqi_bf, wkw_bf, kn_g, kn_b, *, tm):
    n, D = x2.shape
    full = lambda shp: pl.BlockSpec(shp, lambda i: tuple(0 for _ in shp))
    rows = lambda c: pl.BlockSpec((tm, c), lambda i: (i, 0))
    sds = jax.ShapeDtypeStruct
    return pl.pallas_call(
        _proj_kernel,
        grid=(n // tm,),
        in_specs=[rows(D), full((D, 3 * D_MODEL)), full((D, D_QI)), full((D, LANES)),
                  full((1, IDX_DIM)), full((1, IDX_DIM))],
        out_specs=[rows(D), rows(D), rows(D), rows(D), rows(D),
                   pl.BlockSpec((IDX_HEADS, tm, IDX_DIM), lambda i: (0, i, 0)),
                   rows(IDX_DIM), rows(IDX_HEADS)],
        out_shape=[sds((n, D), BF16), sds((n, D), F32), sds((n, D), F32), sds((n, D), BF16),
                   sds((n, D), BF16), sds((IDX_HEADS, n, IDX_DIM), BF16),
                   sds((n, IDX_DIM), F32), sds((n, IDX_HEADS), F32)],
        compiler_params=pltpu.CompilerParams(dimension_semantics=("arbitrary",),
                                             vmem_limit_bytes=VMEM_LIMIT),
        name="attn_in_proj",
    )(x2, wqkv_bf, wqi_bf, wkw_bf, kn_g, kn_b)


def _proj_t_kernel(x_ref, wqt_ref, wk_ref, wv_ref, wvt_ref, wqit_ref, wkw_ref, wkwt_ref, kng_ref, knb_ref,
                   qt_ref, k_ref, v_ref, kb_ref, vt_ref, qit_ref, ki_ref, wtt_ref):
    xb = x_ref[...].astype(BF16)
    qt_ref[...] = (_dot_nt(wqt_ref[...], xb) * Q_SCALE).astype(BF16)
    hk = jnp.dot(xb, wk_ref[...], preferred_element_type=F32)
    k_ref[...] = hk
    kb_ref[...] = hk.astype(BF16)
    v_ref[...] = jnp.dot(xb, wv_ref[...], preferred_element_type=F32)
    vt_ref[...] = _dot_nt(wvt_ref[...], xb).astype(BF16)
    qit_ref[...] = _dot_nt(wqit_ref[...], xb).astype(BF16)
    hkw = jnp.dot(xb, wkw_ref[...], preferred_element_type=F32)
    ki_ref[...] = _ln(hkw[:, :IDX_DIM], kng_ref[...], knb_ref[...])
    hkwt = _dot_nt(wkwt_ref[...], xb)
    wtt_ref[...] = hkwt[IDX_DIM:IDX_DIM + IDX_HEADS, :] * (IDX_HEADS ** -0.5)


N_PROJ_T_INPUTS = 10


def _proj_t_kernel_into(*refs, n_aliased):
    ins = refs[:N_PROJ_T_INPUTS]
    qt_ref, k_ref, v_ref, *outs = refs[N_PROJ_T_INPUTS + n_aliased:]
    _proj_t_kernel(*ins, qt_ref, k_ref.at[0], v_ref.at[0], *outs)


def _proj_t_layer(x2, w_in, kn_g, kn_b, *, tm, layer, n_layers, kv_all=None):
    n, D = x2.shape
    wq, wk, wv = (w_in[:, p * D_MODEL:(p + 1) * D_MODEL] for p in range(3))
    wqi = w_in[:, 3 * D_MODEL:3 * D_MODEL + D_QI]
    wkw = jnp.pad(w_in[:, 3 * D_MODEL + D_QI:], ((0, 0), (0, LANES - IDX_DIM - IDX_HEADS)))
    bf = lambda a: a.astype(BF16)
    weights = [bf(wq.T), bf(wk), bf(wv), bf(wv.T), bf(wqi.T), bf(wkw), bf(wkw.T)]
    full = lambda shp: pl.BlockSpec(shp, lambda i: tuple(0 for _ in shp))
    rows = lambda c: pl.BlockSpec((tm, c), lambda i: (i, 0))
    cols = lambda r: pl.BlockSpec((r, tm), lambda i: (0, i))
    slab = lambda: pl.BlockSpec((1, tm, D), lambda i: (layer, i, 0))
    sds = jax.ShapeDtypeStruct
    in_specs = [rows(D)] + [full(w.shape) for w in weights] + [full((1, IDX_DIM)), full((1, IDX_DIM))]
    operands = [x2, *weights, kn_g, kn_b]
    aliases = {}
    if kv_all is not None:
        aliases = {len(operands): 1, len(operands) + 1: 2}
        in_specs += [pl.BlockSpec(memory_space=pl.ANY), pl.BlockSpec(memory_space=pl.ANY)]
        operands += list(kv_all)
    return pl.pallas_call(
        functools.partial(_proj_t_kernel_into, n_aliased=len(aliases)),
        grid=(n // tm,),
        in_specs=in_specs,
        out_specs=[cols(D), slab(), slab(), rows(D), cols(D), cols(D_QI), rows(IDX_DIM), cols(IDX_HEADS)],
        out_shape=[sds((D, n), BF16), sds((n_layers, n, D), F32), sds((n_layers, n, D), F32), sds((n, D), BF16),
                   sds((D, n), BF16), sds((D_QI, n), BF16), sds((n, IDX_DIM), F32), sds((IDX_HEADS, n), F32)],
        input_output_aliases=aliases,
        compiler_params=pltpu.CompilerParams(dimension_semantics=("arbitrary",),
                                             vmem_limit_bytes=VMEM_LIMIT),
        name="attn_in_proj_t",
    )(*operands)


def _oproj_kernel(o_ref, x_ref, wo_ref, g_ref, b_ref, y_ref):
    h = jnp.dot(o_ref[...], wo_ref[...], preferred_element_type=F32)
    y_ref[...] = _ln(ALPHA * x_ref[...] + h, g_ref[...], b_ref[...])


def _oproj_layer(o2, x2, wo_bf, g, b, *, tm):
    n, D = x2.shape
    row = lambda: pl.BlockSpec((1, D), lambda i: (0, 0))
    return pl.pallas_call(
        _oproj_kernel,
        grid=(n // tm,),
        in_specs=[pl.BlockSpec((tm, D), lambda i: (i, 0)), pl.BlockSpec((tm, D), lambda i: (i, 0)),
                  pl.BlockSpec((D, D), lambda i: (0, 0)), row(), row()],
        out_specs=pl.BlockSpec((tm, D), lambda i: (i, 0)),
        out_shape=jax.ShapeDtypeStruct((n, D), F32),
        compiler_params=pltpu.CompilerParams(dimension_semantics=("arbitrary",)),
        name="attn_out_proj_ln",
    )(o2, x2, wo_bf, g, b)


def _search_threshold(count_ge, count_tie, rmax, rmin, ncols, ksel):
    inf = jnp.inf
    shape = rmax.shape

    def status(flo, fhi, clo):
        mid = 0.5 * flo + 0.5 * fhi
        lo_inf = flo == -inf
        hi_inf = fhi == inf
        p = mid
        p = jnp.where((flo == 0.0) & (fhi > F32_TINY), F32_TINY, p)
        p = jnp.where((flo < 0.0) & (fhi > 0.0), 0.0, p)
        p = jnp.where(lo_inf, jnp.where(fhi > rmin, rmin, -F32_MAX), p)
        p = jnp.where(hi_inf, rmax, p)
        adjacent = ~lo_inf & ~hi_inf & ((mid <= flo) | (mid >= fhi))
        done = (clo == ksel) | adjacent | (hi_inf & (flo >= rmax)) | (lo_inf & (fhi <= -F32_MAX))
        return jnp.where(done, 1.0, 0.0), p

    def search_body(st):
        it, flo, fhi, clo, chi, donef, p, _ = st
        cnt = count_ge(p)
        live = donef < 0.5
        up_lo = live & (cnt >= ksel)
        up_hi = live & (cnt < ksel)
        flo, clo = jnp.where(up_lo, p, flo), jnp.where(up_lo, cnt, clo)
        fhi, chi = jnp.where(up_hi, p, fhi), jnp.where(up_hi, cnt, chi)
        donef, p = status(flo, fhi, clo)
        return it + 1, flo, fhi, clo, chi, donef, p, (jnp.min(donef) > 0.5).astype(jnp.int32)

    flo0 = jnp.full(shape, -inf, F32)
    fhi0 = jnp.full(shape, inf, F32)
    clo0 = jnp.zeros(shape, F32) + jnp.asarray(ncols).astype(F32)
    done0, p0 = status(flo0, fhi0, clo0)
    st = lax.while_loop(lambda st: (st[0] < SEARCH_MAX_STEPS) & (st[7] == 0), search_body,
                        (jnp.int32(0), flo0, fhi0, clo0, jnp.zeros(shape, F32), done0, p0, jnp.int32(0)))
    _, thr, _, clo, chi, _, _, _ = st

    tie_rows = clo > ksel
    need = ksel - chi
    ncols_i = jnp.asarray(ncols).astype(jnp.int32)

    def tie_phase():
        def tb(_, st):
            jlo, jhi = st
            mid = (jlo + jhi) >> 1
            ok = count_tie(thr, mid) >= need
            return jnp.where(ok, jlo, mid), jnp.where(ok, mid, jhi)
        init_j = (jnp.full(shape, -1, jnp.int32), jnp.zeros(shape, jnp.int32) + (ncols_i - 1))
        _, jhi = lax.fori_loop(0, 14, tb, init_j)
        return jnp.where(tie_rows, jhi, ncols_i)

    any_tie = jnp.max(jnp.where(tie_rows, 1.0, 0.0)) > 0.0
    jcut = lax.cond(any_tie, tie_phase, lambda: jnp.zeros(shape, jnp.int32) + ncols_i)
    return thr, jcut


SUBLANES = 8
NACC = 4


def _select_bias_t(sc_ref, n_chunks, cw, tpos, ksel):
    nq = sc_ref.shape[1]
    grp = cw // SUBLANES
    inf = jnp.inf
    srow = lax.broadcasted_iota(jnp.int32, (SUBLANES, nq), 0)

    def tile(v):
        return jnp.broadcast_to(v, (SUBLANES, nq))

    def fold(tile_fn, init):
        def body(c, accs):
            accs = list(accs)
            for g in range(grp):
                first = pl.multiple_of(c * cw + g * SUBLANES, SUBLANES)
                accs[g % NACC] = tile_fn(accs[g % NACC], sc_ref[pl.ds(first, SUBLANES), :], first)
            return tuple(accs)
        return lax.fori_loop(0, n_chunks, body, tuple(init for _ in range(NACC)))

    def finish(parts, op2, op):
        acc = parts[0]
        for part in parts[1:]:
            acc = op2(acc, part)
        return op(acc, axis=0, keepdims=True)

    def count(pred):
        parts = fold(lambda a, x, first: a + pred(x, first), jnp.zeros((SUBLANES, nq), F32))
        return finish(parts, jnp.add, jnp.sum)

    parts = fold(lambda a, x, first: (jnp.maximum(a[0], x), jnp.minimum(a[1], jnp.where(x == -inf, inf, x))),
                 (jnp.full((SUBLANES, nq), -inf, F32), jnp.full((SUBLANES, nq), inf, F32)))
    rmax = finish([p[0] for p in parts], jnp.maximum, jnp.max)
    rmin = finish([p[1] for p in parts], jnp.minimum, jnp.min)

    def count_ge(p):
        pb = tile(p)
        return count(lambda x, first: jnp.where(x >= pb, 1.0, 0.0))

    def count_tie(t, j):
        tb, jb = tile(t), tile(j)
        return count(lambda x, first: jnp.where(x == tb, jnp.where(srow + first <= jb, 1.0, 0.0), 0.0))

    thr, jcut = _search_threshold(count_ge, count_tie, rmax, rmin, n_chunks * cw, ksel)
    tb, jb = tile(thr), tile(jnp.minimum(jcut, tpos))

    krow = (lax.broadcasted_iota(jnp.int32, (grp, SUBLANES, nq), 0) * SUBLANES
            + lax.broadcasted_iota(jnp.int32, (grp, SUBLANES, nq), 1))

    def write(c, carry):
        rows = pl.ds(pl.multiple_of(c * cw, cw), cw)
        x = sc_ref[rows, :].reshape(grp, SUBLANES, nq)
        tie_sel = jnp.where(krow + c * cw <= jb[None], 0.0, NEG_BIAS)
        bias = jnp.where(x > tb[None], 0.0, jnp.where(x == tb[None], tie_sel, NEG_BIAS))
        sc_ref[rows, :] = bias.reshape(cw, nq)
        return carry
    lax.fori_loop(0, n_chunks, write, 0)


def _select_bias(sc_ref, r0, nr, n_chunks, cw, tpos, ksel):
    sub = min(nr, SEL_SUB)
    nsub = nr // sub
    ncols = n_chunks * cw
    nt = cw // LANES
    lane = lax.broadcasted_iota(jnp.int32, (sub, LANES), 1)
    inf = jnp.inf

    def fold(tile_fn, init, cols=(), store=False):
        res = []
        for r in range(nsub):
            rows = slice(r0 + r * sub, r0 + (r + 1) * sub)
            bc = [jnp.broadcast_to(c[:, r * sub:(r + 1) * sub], (LANES, sub)).T for c in cols]

            def body(c, accs, rows=rows, bc=bc):
                cs = pl.ds(pl.multiple_of(c * cw, cw), cw)
                x = sc_ref[rows, cs]
                outs = []
                for j in range(nt):
                    accs = tile_fn(accs, x[:, j * LANES:(j + 1) * LANES], c * cw + j * LANES, bc)
                    if store:
                        outs.append(accs)
                if store:
                    sc_ref[rows, cs] = jnp.concatenate(outs, axis=1) if nt > 1 else outs[0]
                    return 0
                return accs
            res.append(lax.fori_loop(0, n_chunks, body, init))
        return res

    def row_reduce(parts, k, op):
        return jnp.concatenate([op(p[k].T, axis=0, keepdims=True) for p in parts], axis=1)

    def count(pred, cols):
        parts = fold(lambda a, x, base, bc: (a[0] + pred(x, base, bc),), (jnp.zeros((sub, LANES), F32),), cols)
        return row_reduce(parts, 0, jnp.sum)

    parts = fold(lambda a, x, base, bc: (jnp.maximum(a[0], x), jnp.minimum(a[1], jnp.where(x == -inf, inf, x))),
                 (jnp.full((sub, LANES), -inf, F32), jnp.full((sub, LANES), inf, F32)))
    rmax = row_reduce(parts, 0, jnp.max)
    rmin = row_reduce(parts, 1, jnp.min)

    thr, jcut = _search_threshold(
        lambda p: count(lambda x, base, bc: jnp.where(x >= bc[0], 1.0, 0.0), (p,)),
        lambda t, j: count(lambda x, base, bc: jnp.where(x == bc[0], jnp.where(lane + base <= bc[1], 1.0, 0.0), 0.0),
                           (t, j)),
        rmax, rmin, ncols, ksel)

    def bias_tile(_, x, base, bc):
        tie_sel = jnp.where(lane + base <= bc[1], 0.0, NEG_BIAS)
        return jnp.where(x > bc[0], 0.0, jnp.where(x == bc[0], tie_sel, NEG_BIAS))
    fold(bias_tile, 0, (thr, jnp.minimum(jcut, tpos)), store=True)


SEL_ROWS = 256


def _attn_prompt_t_kernel(qmap_ref, kmap_ref, qt_ref, qit_ref, wtt_ref, ki_ref, k_ref, vt_ref, o_ref,
                          sc_ref, wb_ref, m_ref, l_ref, a_ref, acc_ref, s_ref, p_ref, *, tq, tk, ksel):
    n = pl.program_id(1)
    i = qmap_ref[n]
    kj = kmap_ref[n]
    last = ((i + 1) * tq - 1) // tk
    grp = tk // SUBLANES

    def keyred(x, op):
        return op(op(x.reshape(grp // NACC, NACC, SUBLANES, tq), axis=0), axis=0)

    def rep(v8, rows):
        return jnp.concatenate([v8] * (rows // SUBLANES), axis=0)

    @pl.when(kj == 0)
    def _():
        wts = wtt_ref[...] * (IDX_DIM ** -0.5)
        for h in range(IDX_HEADS):
            wb_ref[h] = jnp.broadcast_to(wts[h:h + 1, :], (SUBLANES, tq))
        tpos = i * tq + lax.broadcasted_iota(jnp.int32, (1, tq), 1)

        def chunk_scores(c, masked):
            rows = pl.ds(pl.multiple_of(c * tk, tk), tk)
            kic = ki_ref[rows, :].astype(BF16)
            sc = None
            for h in range(IDX_HEADS):
                d = jnp.dot(kic, qit_ref[h * IDX_DIM:(h + 1) * IDX_DIM, :], preferred_element_type=F32)
                d = jnp.maximum(d, 0.0) * rep(wb_ref[h], tk)
                sc = d if sc is None else sc + d
            if masked:
                kidx = c * tk + lax.broadcasted_iota(jnp.int32, (tk, tq), 0)
                sc = jnp.where(kidx <= tpos, sc, -jnp.inf)
            sc_ref[rows, :] = sc

        def body(c, carry):
            chunk_scores(c, False)
            return carry
        lax.fori_loop(0, last, body, 0)
        chunk_scores(last, True)

        _select_bias_t(sc_ref, last + 1, tk, tpos, ksel)
        m_ref[...] = jnp.full(m_ref.shape, NEG_BIAS, F32)
        l_ref[...] = jnp.zeros(l_ref.shape, F32)
        acc_ref[...] = jnp.zeros(acc_ref.shape, F32)

    rows = pl.ds(pl.multiple_of(kj * tk, tk), tk)
    heads = [slice(h * HEAD_DIM, (h + 1) * HEAD_DIM) for h in range(N_HEADS)]
    for h, hs in enumerate(heads):
        s_ref[h] = jnp.dot(k_ref[:, hs], qt_ref[hs, :], preferred_element_type=F32) + sc_ref[rows, :]
    for h, hs in enumerate(heads):
        m_old = m_ref[h]
        smax = jnp.max(keyred(s_ref[h], jnp.max), axis=0, keepdims=True)
        m_new = jnp.maximum(m_old, jnp.broadcast_to(smax, (SUBLANES, tq)))
        a_ref[h] = jnp.exp2(m_old - m_new)
        m_ref[h] = m_new
    for h, hs in enumerate(heads):
        p = jnp.exp2(s_ref[h] - rep(m_ref[h], tk))
        l_ref[h] = a_ref[h] * l_ref[h] + keyred(p, jnp.sum)
        p_ref[h] = p.astype(BF16)
    for h, hs in enumerate(heads):
        acc_ref[h] = rep(a_ref[h], HEAD_DIM) * acc_ref[h] + jnp.dot(vt_ref[hs, :], p_ref[h],
                                                                     preferred_element_type=F32)

    @pl.when(kj == last)
    def _():
        for h, hs in enumerate(heads):
            l = jnp.broadcast_to(jnp.sum(l_ref[h], axis=0, keepdims=True), (SUBLANES, tq))
            o_ref[:, hs] = (acc_ref[h] / rep(l, HEAD_DIM)).T.astype(BF16)


def _attn_prompt_t(B, qt, qit, wtt, ki, kb, vt, *, tq=512, tk=512):
    n, D = kb.shape
    T = n // B
    ksel = min(TOPK_MAX, T // 4)
    nq, nk = T // tq, T // tk
    qmap, kmap = [], []
    for i in range(nq):
        for kj in range(((i + 1) * tq - 1) // tk + 1):
            qmap.append(i)
            kmap.append(kj)
    qmap = jnp.asarray(np.asarray(qmap, np.int32))
    kmap = jnp.asarray(np.asarray(kmap, np.int32))
    qcol = lambda r: pl.BlockSpec((r, tq), lambda b, s, qm, km: (0, b * nq + qm[s]))
    grid_spec = pltpu.PrefetchScalarGridSpec(
        num_scalar_prefetch=2,
        grid=(B, int(qmap.shape[0])),
        in_specs=[qcol(D), qcol(D_QI), qcol(IDX_HEADS),
                  pl.BlockSpec((T, IDX_DIM), lambda b, s, qm, km: (b, 0)),
                  pl.BlockSpec((tk, D), lambda b, s, qm, km: (b * nk + km[s], 0)),
                  pl.BlockSpec((D, tk), lambda b, s, qm, km: (0, b * nk + km[s]))],
        out_specs=pl.BlockSpec((tq, D), lambda b, s, qm, km: (b * nq + qm[s], 0)),
        scratch_shapes=[pltpu.VMEM((T, tq), F32),
                        pltpu.VMEM((IDX_HEADS, SUBLANES, tq), F32),
                        pltpu.VMEM((N_HEADS, SUBLANES, tq), F32),
                        pltpu.VMEM((N_HEADS, SUBLANES, tq), F32),
                        pltpu.VMEM((N_HEADS, SUBLANES, tq), F32),
                        pltpu.VMEM((N_HEADS, HEAD_DIM, tq), F32),
                        pltpu.VMEM((N_HEADS, tk, tq), F32),
                        pltpu.VMEM((N_HEADS, tk, tq), BF16)])
    return pl.pallas_call(
        functools.partial(_attn_prompt_t_kernel, tq=tq, tk=tk, ksel=ksel),
        grid_spec=grid_spec,
        out_shape=jax.ShapeDtypeStruct((n, D), BF16),
        compiler_params=pltpu.CompilerParams(dimension_semantics=("arbitrary", "arbitrary"),
                                             vmem_limit_bytes=VMEM_LIMIT),
        name="attn_prompt_t",
    )(qmap, kmap, qt, qit, wtt, ki, kb, vt)


KI_PAGES_PER_STEP = 16
KV_PAGES_PER_STEP = 8
SEL_CHUNK = 5 * LANES


def _sample_scores_kernel(pt_ref, qi_ref, wt_ref, kin_ref, *rest, n_pages, tq):
    del pt_ref
    g1 = KI_PAGES_PER_STEP
    kidx_refs, sc_ref = rest[:g1], rest[g1]
    s = pl.program_id(1)
    wcol = jnp.broadcast_to(wt_ref[0] * (IDX_DIM ** -0.5), (IDX_HEADS * tq, PAGE_SIZE))

    def chunk_scores(kit_chunk):
        d = jnp.maximum(jnp.dot(qi_ref[0], kit_chunk.astype(BF16), preferred_element_type=F32), 0.0) * wcol
        sc = d[0:tq]
        for h in range(1, IDX_HEADS):
            sc = sc + d[h * tq:(h + 1) * tq]
        return sc

    for g in range(g1):
        page = s * g1 + g
        sc_ref[0, :, pl.ds(pl.multiple_of(page * PAGE_SIZE, PAGE_SIZE), PAGE_SIZE)] = chunk_scores(kidx_refs[g][0, 0])

    @pl.when(s == n_pages // g1 - 1)
    def _():
        qrow = lax.broadcasted_iota(jnp.int32, (tq, PAGE_SIZE), 0)
        jcol = lax.broadcasted_iota(jnp.int32, (tq, PAGE_SIZE), 1)
        sc_ref[0, :, n_pages * PAGE_SIZE:] = jnp.where(jcol <= qrow, chunk_scores(kin_ref[0]), -jnp.inf)


def _sample_scores(layer, page_table, qi, wt, ki_new, cache_kidx, *, tq):
    Bd, n_pages = page_table.shape
    g1 = KI_PAGES_PER_STEP
    ncols = (n_pages + 1) * PAGE_SIZE

    def per_b(shape):
        return pl.BlockSpec((1,) + shape, lambda b, s, pt: (b,) + tuple(0 for _ in shape))

    def kidx_spec(g):
        return pl.BlockSpec((1, 1, IDX_DIM, PAGE_SIZE), lambda b, s, pt: (layer, pt[b, s * g1 + g], 0, 0))

    grid_spec = pltpu.PrefetchScalarGridSpec(
        num_scalar_prefetch=1,
        grid=(Bd, n_pages // g1),
        in_specs=[per_b((IDX_HEADS * tq, IDX_DIM)), per_b((IDX_HEADS * tq, 1)), per_b((IDX_DIM, PAGE_SIZE))]
                 + [kidx_spec(g) for g in range(g1)],
        out_specs=per_b((tq, ncols)))
    return pl.pallas_call(
        functools.partial(_sample_scores_kernel, n_pages=n_pages, tq=tq),
        grid_spec=grid_spec,
        out_shape=jax.ShapeDtypeStruct((Bd, tq, ncols), F32),
        compiler_params=pltpu.CompilerParams(dimension_semantics=("arbitrary", "arbitrary")),
        name="sample_scores",
    )(page_table, qi, wt, ki_new, *([cache_kidx] * g1))


def _sample_select_kernel(sc_ref, o_ref, *, tq, past, ksel):
    o_ref[...] = sc_ref[...]
    nr, ncols = o_ref.shape
    tpos = past + lax.rem(lax.broadcasted_iota(jnp.int32, (1, nr), 1), tq)
    _select_bias(o_ref, 0, nr, ncols // SEL_CHUNK, SEL_CHUNK, tpos, ksel)


def _sample_select(sc2, *, tq, past, ksel):
    n, ncols = sc2.shape
    nr = min(SEL_ROWS, n)
    return pl.pallas_call(
        functools.partial(_sample_select_kernel, tq=tq, past=past, ksel=ksel),
        grid=(n // nr,),
        in_specs=[pl.BlockSpec((nr, ncols), lambda i: (i, 0))],
        out_specs=pl.BlockSpec((nr, ncols), lambda i: (i, 0)),
        out_shape=jax.ShapeDtypeStruct((n, ncols), F32),
        compiler_params=pltpu.CompilerParams(dimension_semantics=("arbitrary",), vmem_limit_bytes=VMEM_LIMIT),
        name="sample_select",
    )(sc2)


def _sample_attend_kernel(pt_ref, q_ref, bias_ref, kn_ref, vn_ref, *rest, n_pages, tq):
    del pt_ref
    g2 = KV_PAGES_PER_STEP
    kpage_refs, vpage_refs = rest[:g2], rest[g2:2 * g2]
    o_ref, s_ref, l_ref, acc_ref = rest[2 * g2:]
    nsk = n_pages // g2
    s = pl.program_id(1)
    new_cols = slice(n_pages * PAGE_SIZE, (n_pages + 1) * PAGE_SIZE)

    def head_rows(h):
        return slice(h * tq, (h + 1) * tq)

    def head_cols(h):
        return slice(h * HEAD_DIM, (h + 1) * HEAD_DIM)

    @pl.when(s < nsk)
    def _():
        for g in range(g2):
            cols = pl.ds(pl.multiple_of((s * g2 + g) * PAGE_SIZE, PAGE_SIZE), PAGE_SIZE)
            bias = bias_ref[0, :, cols]
            for h in range(N_HEADS):
                kh = kpage_refs[g][0, 0, pl.ds(h, PAGE_SIZE, stride=N_HEADS), :].astype(BF16)
                s_ref[head_rows(h), cols] = _dot_nt(q_ref[0, :, head_cols(h)], kh) + bias

    @pl.when(s == nsk - 1)
    def _():
        bias = bias_ref[0, :, new_cols]
        for h in range(N_HEADS):
            s_ref[head_rows(h), new_cols] = _dot_nt(q_ref[0, :, head_cols(h)], kn_ref[0, :, head_cols(h)]) + bias
        sall = s_ref[...]
        p = jnp.exp2(sall - jnp.max(sall, axis=1, keepdims=True))
        l_ref[...] = jnp.sum(p, axis=1, keepdims=True)
        s_ref[...] = p
        acc_ref[...] = jnp.zeros(acc_ref.shape, F32)

    @pl.when(s >= nsk)
    def _():
        for h in range(N_HEADS):
            o = jnp.zeros((tq, HEAD_DIM), F32)
            for g in range(g2):
                cols = pl.ds(pl.multiple_of(((s - nsk) * g2 + g) * PAGE_SIZE, PAGE_SIZE), PAGE_SIZE)
                vh = vpage_refs[g][0, 0, pl.ds(h, PAGE_SIZE, stride=N_HEADS), :].astype(BF16)
                o = o + jnp.dot(s_ref[head_rows(h), cols].astype(BF16), vh, preferred_element_type=F32)
            acc_ref[:, head_cols(h)] += o

    @pl.when(s == 2 * nsk - 1)
    def _():
        for h in range(N_HEADS):
            o = acc_ref[:, head_cols(h)] + jnp.dot(s_ref[head_rows(h), new_cols].astype(BF16),
                                                   vn_ref[0, :, head_cols(h)], preferred_element_type=F32)
            o_ref[0, :, head_cols(h)] = (o / l_ref[head_rows(h), :]).astype(BF16)


def _sample_attend(layer, page_table, q, bias, kb_new, vb_new, cache_k, cache_v):
    Bd, tq, D = q.shape
    n_pages = page_table.shape[1]
    ncols = (n_pages + 1) * PAGE_SIZE
    g2 = KV_PAGES_PER_STEP
    nsk = n_pages // g2

    def per_b(shape):
        return pl.BlockSpec((1,) + shape, lambda b, s, pt: (b,) + tuple(0 for _ in shape))

    def page_spec(g, first_step):
        return pl.BlockSpec((1, 1, PAGE_SIZE * N_HEADS, HEAD_DIM),
                            lambda b, s, pt: (layer, pt[b, jnp.clip(s - first_step, 0, nsk - 1) * g2 + g], 0, 0))

    grid_spec = pltpu.PrefetchScalarGridSpec(
        num_scalar_prefetch=1,
        grid=(Bd, 2 * nsk),
        in_specs=[per_b((tq, D)), per_b((tq, ncols)), per_b((PAGE_SIZE, D)), per_b((PAGE_SIZE, D))]
                 + [page_spec(g, 0) for g in range(g2)] + [page_spec(g, nsk) for g in range(g2)],
        out_specs=per_b((tq, D)),
        scratch_shapes=[pltpu.VMEM((N_HEADS * tq, ncols), F32),
                        pltpu.VMEM((N_HEADS * tq, 1), F32),
                        pltpu.VMEM((tq, D), F32)])
    return pl.pallas_call(
        functools.partial(_sample_attend_kernel, n_pages=n_pages, tq=tq),
        grid_spec=grid_spec,
        out_shape=jax.ShapeDtypeStruct((Bd, tq, D), BF16),
        compiler_params=pltpu.CompilerParams(dimension_semantics=("arbitrary", "arbitrary"),
                                             vmem_limit_bytes=VMEM_LIMIT),
        name="sample_attend",
    )(page_table, q, bias, kb_new, vb_new, *([cache_k] * g2), *([cache_v] * g2))


def _pad_rows(a, n):
    return jnp.pad(a, ((0, 0), (0, n - a.shape[1]), (0, 0)))


def kernel(x_prompt, x_sample, state_pool, cache_k, cache_v, cache_kidx, page_table, pool_w, pool_scale,
           attn_w_in, attn_kn_g, attn_kn_b, attn_w_o, mlp_w1, mlp_w2, ln_g, ln_b):
    B, T, D = x_prompt.shape
    Bd, Td, _ = x_sample.shape
    n_attn, n_phys = cache_k.shape[0], cache_k.shape[1]
    n_pages = page_table.shape[1]
    past = n_pages * PAGE_SIZE
    xp, xs = x_prompt, x_sample
    ck = cache_k.reshape(n_attn, n_phys, PAGE_SIZE * N_HEADS, HEAD_DIM)
    cv = cache_v.reshape(n_attn, n_phys, PAGE_SIZE * N_HEADS, HEAD_DIM)
    ckit = cache_kidx.transpose(0, 1, 3, 2)
    pool_p, pool_s = [], []
    kip, ksm, vsm, kism = [], [], [], []
    kv_prompt = None
    for i in range(DEPTH):
        j = i // 2
        g0, b0 = ln_g[i, 0][None], ln_b[i, 0][None]
        g1, b1 = ln_g[i, 1][None], ln_b[i, 1][None]
        if i % 2 == 0:
            w_bf = pool_w[j].astype(BF16)
            scale = pool_scale[j][None]
            pool_p.append(xp[:, T - POOL_BUF:])
            halo_s = jnp.concatenate([jnp.zeros((Bd, 1, D), xs.dtype), state_pool[j].astype(xs.dtype)], axis=1)
            pool_s.append(jnp.concatenate([halo_s, xs], axis=1)[:, -POOL_BUF:])
            xp = _pool_layer(xp, xp, w_bf, scale, g0, b0, tq=512, start=0, first_is_zero=True)
            xs = _pool_layer(xs, halo_s, w_bf, scale, g0, b0, tq=Td, start=past, first_is_zero=False)
        else:
            w_in = attn_w_in[j]
            wqkv = w_in[:, :3 * D_MODEL].astype(BF16)
            wqi = w_in[:, 3 * D_MODEL:3 * D_MODEL + D_QI].astype(BF16)
            wkw = jnp.pad(w_in[:, 3 * D_MODEL + D_QI:], ((0, 0), (0, LANES - IDX_DIM - IDX_HEADS))).astype(BF16)
            kng, knb = attn_kn_g[j][None], attn_kn_b[j][None]
            wo = attn_w_o[j].astype(BF16)

            qt, k_all, v_all, kb, vt, qit, ki, wtt = _proj_t_layer(
                xp.reshape(B * T, D), w_in, kng, knb, tm=512, layer=j, n_layers=n_attn, kv_all=kv_prompt)
            kv_prompt = (k_all, v_all)
            o = _attn_prompt_t(B, qt, qit, wtt, ki, kb, vt)
            xp = _oproj_layer(o, xp.reshape(B * T, D), wo, g0, b0, tm=512).reshape(B, T, D)
            kip.append(ki.reshape(B, T, IDX_DIM))

            n = Bd * Td
            q, k, v, kb, vb, qi, ki, wt = _proj_layer(xs.reshape(n, D), wqkv, wqi, wkw, kng, knb, tm=n)
            qi_s = qi.reshape(IDX_HEADS, Bd, Td, IDX_DIM).transpose(1, 0, 2, 3).reshape(Bd, IDX_HEADS * Td, IDX_DIM)
            wt_s = wt.reshape(Bd, Td, IDX_HEADS).transpose(0, 2, 1).reshape(Bd, IDX_HEADS * Td, 1)
            kit_new = _pad_rows(ki.reshape(Bd, Td, IDX_DIM), PAGE_SIZE).transpose(0, 2, 1)
            sc = _sample_scores(j, page_table, qi_s, wt_s, kit_new, ckit, tq=Td)
            bias = _sample_select(sc.reshape(n, sc.shape[-1]), tq=Td, past=past,
                                  ksel=min(TOPK_MAX, (past + Td) // 4))
            o = _sample_attend(j, page_table, q.reshape(Bd, Td, D), bias.reshape(Bd, Td, -1),
                               _pad_rows(kb.reshape(Bd, Td, D), PAGE_SIZE),
                               _pad_rows(vb.reshape(Bd, Td, D), PAGE_SIZE), ck, cv)
            xs = _oproj_layer(o.reshape(n, D), xs.reshape(n, D), wo, g0, b0, tm=n).reshape(Bd, Td, D)
            ksm.append(k.reshape(Bd, Td, N_HEADS, HEAD_DIM))
            vsm.append(v.reshape(Bd, Td, N_HEADS, HEAD_DIM))
            kism.append(ki.reshape(Bd, Td, IDX_DIM))
        w1, w2 = mlp_w1[i].astype(BF16), mlp_w2[i].astype(BF16)
        xp = _mlp_layer(xp.reshape(B * T, D), w1, w2, g1, b1, tm=512).reshape(B, T, D)
        xs = _mlp_layer(xs.reshape(Bd * Td, D), w1, w2, g1, b1, tm=Bd * Td).reshape(Bd, Td, D)
    k_prompt, v_prompt = (a.reshape(n_attn, B, T, N_HEADS, HEAD_DIM) for a in kv_prompt)
    return (xp, xs, jnp.stack(pool_p), jnp.stack(pool_s), k_prompt, v_prompt, jnp.stack(kip),
            jnp.stack(ksm), jnp.stack(vsm), jnp.stack(kism))
```

```python
import functools
import math

import numpy as np
import jax
import jax.numpy as jnp
from jax import lax
from jax.experimental import pallas as pl
from jax.experimental.pallas import tpu as pltpu

D_MODEL = 1024
DEPTH = 4
PAST_LEN = 8192
PAGE_SIZE = 128
POOL_WINDOWS = (2, 4, 8, 16)
POOL_GROUP = D_MODEL // len(POOL_WINDOWS)
POOL_BUF = max(POOL_WINDOWS) - 1
HALO = POOL_BUF + 1
N_HEADS = 8
HEAD_DIM = D_MODEL // N_HEADS
IDX_HEADS = 8
IDX_DIM = 64
TOPK_MAX = 256
D_FF = 4 * D_MODEL
ALPHA = (2 * DEPTH) ** 0.25
LN_EPS = 1e-5
D_QI = IDX_HEADS * IDX_DIM
Q_SCALE = HEAD_DIM ** -0.5 * math.log2(math.e)

LANES = 128
NEG_BIAS = -1e30
F32_MAX = float(np.finfo(np.float32).max)
F32_TINY = float(np.finfo(np.float32).tiny)
VMEM_LIMIT = 52 * 1024 * 1024
SEARCH_MAX_STEPS = 400
SEL_SUB = 128

F32 = jnp.float32
BF16 = jnp.bfloat16


def _ln(y, g, b):
    mu = jnp.mean(y, axis=-1, keepdims=True)
    yc = y - mu
    var = jnp.mean(yc * yc, axis=-1, keepdims=True)
    return yc * lax.rsqrt(var + LN_EPS) * g + b


def _dot_nt(a, b):
    return lax.dot_general(a, b, (((1,), (1,)), ((), ())), preferred_element_type=F32)


def _pool_kernel(x_ref, halo_ref, w_ref, sc_ref, g_ref, b_ref, o_ref, *, tq, start, first_is_zero):
    i = pl.program_id(1)
    x = x_ref[0]
    halo = halo_ref[0]
    if first_is_zero:
        halo = jnp.where(i == 0, 0.0, halo)
    xa = jnp.concatenate([halo, x], axis=0)
    pos = start + i * tq + lax.broadcasted_iota(jnp.int32, (tq, 1), 0)
    outs = []
    for g, w in enumerate(POOL_WINDOWS):
        sl = slice(g * POOL_GROUP, (g + 1) * POOL_GROUP)
        s = xa[:, sl]
        sh = 1
        while sh < w:
            s = s + pltpu.roll(s, sh, axis=0)
            sh *= 2
        cnt = jnp.minimum(w, pos + 1).astype(F32)
        p = s[HALO:] / cnt - x[:, sl]
        outs.append(jnp.dot(p.astype(BF16), w_ref[g], preferred_element_type=F32))
    y = jnp.concatenate(outs, axis=1) * sc_ref[...]
    o_ref[0] = _ln(ALPHA * x + y, g_ref[...], b_ref[...])


def _pool_layer(x, halo, w_bf, scale, g, b, *, tq, start, first_is_zero):
    B, T, D = x.shape
    nq = T // tq
    if first_is_zero:
        halo_spec = pl.BlockSpec((1, HALO, D), lambda bb, i: (bb, jnp.maximum(i * (tq // HALO) - 1, 0), 0))
    else:
        halo_spec = pl.BlockSpec((1, HALO, D), lambda bb, i: (bb, 0, 0))
    row = lambda: pl.BlockSpec((1, D), lambda bb, i: (0, 0))
    return pl.pallas_call(
        functools.partial(_pool_kernel, tq=tq, start=start, first_is_zero=first_is_zero),
        grid=(B, nq),
        in_specs=[pl.BlockSpec((1, tq, D), lambda bb, i: (bb, i, 0)),
                  halo_spec,
                  pl.BlockSpec((len(POOL_WINDOWS), POOL_GROUP, POOL_GROUP), lambda bb, i: (0, 0, 0)),
                  row(), row(), row()],
        out_specs=pl.BlockSpec((1, tq, D), lambda bb, i: (bb, i, 0)),
        out_shape=jax.ShapeDtypeStruct((B, T, D), F32),
        compiler_params=pltpu.CompilerParams(dimension_semantics=("arbitrary", "arbitrary")),
        name="pool_mix_ln",
    )(x, halo, w_bf, scale, g, b)


def _mlp_kernel(x_ref, w1_ref, w2_ref, g_ref, b_ref, o_ref, *, ffc):
    x = x_ref[...]
    xb = x.astype(BF16)
    acc = jnp.zeros(x.shape, F32)
    for c in range(D_FF // ffc):
        h = jnp.dot(xb, w1_ref[:, c * ffc:(c + 1) * ffc], preferred_element_type=F32)
        h = jnp.maximum(h, 0.0)
        acc = acc + jnp.dot((h * h).astype(BF16), w2_ref[c * ffc:(c + 1) * ffc, :],
                            preferred_element_type=F32)
    o_ref[...] = _ln(ALPHA * x + acc, g_ref[...], b_ref[...])


def _mlp_layer(x2, w1_bf, w2_bf, g, b, *, tm):
    n, D = x2.shape
    row = lambda: pl.BlockSpec((1, D), lambda i: (0, 0))
    return pl.pallas_call(
        functools.partial(_mlp_kernel, ffc=1024),
        grid=(n // tm,),
        in_specs=[pl.BlockSpec((tm, D), lambda i: (i, 0)),
                  pl.BlockSpec((D, D_FF), lambda i: (0, 0), pipeline_mode=pl.Buffered(1)),
                  pl.BlockSpec((D_FF, D), lambda i: (0, 0), pipeline_mode=pl.Buffered(1)),
                  row(), row()],
        out_specs=pl.BlockSpec((tm, D), lambda i: (i, 0)),
        out_shape=jax.ShapeDtypeStruct((n, D), F32),
        compiler_params=pltpu.CompilerParams(dimension_semantics=("arbitrary",),
                                             vmem_limit_bytes=VMEM_LIMIT),
        name="mlp_ln",
    )(x2, w1_bf, w2_bf, g, b)


def _proj_kernel(x_ref, wqkv_ref, wqi_ref, wkw_ref, kng_ref, knb_ref,
                 q_ref, k_ref, v_ref, kb_ref, vb_ref, qi_ref, ki_ref, wt_ref):
    xb = x_ref[...].astype(BF16)
    hq = jnp.dot(xb, wqkv_ref[:, :D_MODEL], preferred_element_type=F32)
    q_ref[...] = (hq * Q_SCALE).astype(BF16)
    for part, (f_ref, b_ref) in enumerate(((k_ref, kb_ref), (v_ref, vb_ref)), start=1):
        h = jnp.dot(xb, wqkv_ref[:, part * D_MODEL:(part + 1) * D_MODEL], preferred_element_type=F32)
        f_ref[...] = h
        b_ref[...] = h.astype(BF16)
    hqi = jnp.dot(xb, wqi_ref[...], preferred_element_type=F32)
    for hh in range(IDX_HEADS):
        qi_ref[hh] = hqi[:, hh * IDX_DIM:(hh + 1) * IDX_DIM].astype(BF16)
    hkw = jnp.dot(xb, wkw_ref[...], preferred_element_type=F32)
    ki_ref[...] = _ln(hkw[:, :IDX_DIM], kng_ref[...], knb_ref[...])
    wt_ref[...] = hkw[:, IDX_DIM:IDX_DIM + IDX_HEADS] * (IDX_HEADS ** -0.5)


def _proj_layer(x2, wqkv_bf, wqi_bf, wkw_bf, kn_g, kn_b, *, tm):
    n, D = x2.shape
    full = lambda shp: pl.BlockSpec(shp, lambda i: tuple(0 for _ in shp))
    rows = lambda c: pl.BlockSpec((tm, c), lambda i: (i, 0))
    sds = jax.ShapeDtypeStruct
    return pl.pallas_call(
        _proj_kernel,
        grid=(n // tm,),
        in_specs=[rows(D), full((D, 3 * D_MODEL)), full((D, D_QI)), full((D, LANES)),
                  full((1, IDX_DIM)), full((1, IDX_DIM))],
        out_specs=[rows(D), rows(D), rows(D), rows(D), rows(D),
                   pl.BlockSpec((IDX_HEADS, tm, IDX_DIM), lambda i: (0, i, 0)),
                   rows(IDX_DIM), rows(IDX_HEADS)],
        out_shape=[sds((n, D), BF16), sds((n, D), F32), sds((n, D), F32), sds((n, D), BF16),
                   sds((n, D), BF16), sds((IDX_HEADS, n, IDX_DIM), BF16),
                   sds((n, IDX_DIM), F32), sds((n, IDX_HEADS), F32)],
        compiler_params=pltpu.CompilerParams(dimension_semantics=("arbitrary",),
                                             vmem_limit_bytes=VMEM_LIMIT),
        name="attn_in_proj",
    )(x2, wqkv_bf, wqi_bf, wkw_bf, kn_g, kn_b)


def _proj_t_kernel(x_ref, wqt_ref, wk_ref, wv_ref, wvt_ref, wqit_ref, wkw_ref, wkwt_ref, kng_ref, knb_ref,
                   qt_ref, k_ref, v_ref, kb_ref, vt_ref, qit_ref, ki_ref, wtt_ref):
    xb = x_ref[...].astype(BF16)
    qt_ref[...] = (_dot_nt(wqt_ref[...], xb) * Q_SCALE).astype(BF16)
    hk = jnp.dot(xb, wk_ref[...], preferred_element_type=F32)
    k_ref[...] = hk
    kb_ref[...] = hk.astype(BF16)
    v_ref[...] = jnp.dot(xb, wv_ref[...], preferred_element_type=F32)
    vt_ref[...] = _dot_nt(wvt_ref[...], xb).astype(BF16)
    qit_ref[...] = _dot_nt(wqit_ref[...], xb).astype(BF16)
    hkw = jnp.dot(xb, wkw_ref[...], preferred_element_type=F32)
    ki_ref[...] = _ln(hkw[:, :IDX_DIM], kng_ref[...], knb_ref[...])
    hkwt = _dot_nt(wkwt_ref[...], xb)
    wtt_ref[...] = hkwt[IDX_DIM:IDX_DIM + IDX_HEADS, :] * (IDX_HEADS ** -0.5)


N_PROJ_T_INPUTS = 10


def _proj_t_kernel_into(*refs):
    ins = refs[:N_PROJ_T_INPUTS]
    qt_ref, k_ref, v_ref, *outs = refs[N_PROJ_T_INPUTS + 2:]
    _proj_t_kernel(*ins, qt_ref, k_ref.at[0], v_ref.at[0], *outs)


def _proj_t_layer(x2, w_in, kn_g, kn_b, *, tm, layer, kv_all):
    n, D = x2.shape
    n_layers = kv_all[0].shape[0]
    wq, wk, wv = (w_in[:, p * D_MODEL:(p + 1) * D_MODEL] for p in range(3))
    wqi = w_in[:, 3 * D_MODEL:3 * D_MODEL + D_QI]
    wkw = jnp.pad(w_in[:, 3 * D_MODEL + D_QI:], ((0, 0), (0, LANES - IDX_DIM - IDX_HEADS)))
    bf = lambda a: a.astype(BF16)
    weights = [bf(wq.T), bf(wk), bf(wv), bf(wv.T), bf(wqi.T), bf(wkw), bf(wkw.T)]
    full = lambda shp: pl.BlockSpec(shp, lambda i: tuple(0 for _ in shp))
    rows = lambda c: pl.BlockSpec((tm, c), lambda i: (i, 0))
    cols = lambda r: pl.BlockSpec((r, tm), lambda i: (0, i))
    slab = lambda: pl.BlockSpec((1, tm, D), lambda i: (layer, i, 0))
    sds = jax.ShapeDtypeStruct
    return pl.pallas_call(
        _proj_t_kernel_into,
        grid=(n // tm,),
        in_specs=[rows(D)] + [full(w.shape) for w in weights] + [full((1, IDX_DIM)), full((1, IDX_DIM))]
                 + [pl.BlockSpec(memory_space=pl.ANY), pl.BlockSpec(memory_space=pl.ANY)],
        out_specs=[cols(D), slab(), slab(), rows(D), cols(D), cols(D_QI), rows(IDX_DIM), cols(IDX_HEADS)],
        out_shape=[sds((D, n), BF16), sds((n_layers, n, D), F32), sds((n_layers, n, D), F32), sds((n, D), BF16),
                   sds((D, n), BF16), sds((D_QI, n), BF16), sds((n, IDX_DIM), F32), sds((IDX_HEADS, n), F32)],
        input_output_aliases={N_PROJ_T_INPUTS: 1, N_PROJ_T_INPUTS + 1: 2},
        compiler_params=pltpu.CompilerParams(dimension_semantics=("arbitrary",),
                                             vmem_limit_bytes=VMEM_LIMIT),
        name="attn_in_proj_t",
    )(x2, *weights, kn_g, kn_b, *kv_all)


def _oproj_kernel(o_ref, x_ref, wo_ref, g_ref, b_ref, y_ref):
    h = jnp.dot(o_ref[...], wo_ref[...], preferred_element_type=F32)
    y_ref[...] = _ln(ALPHA * x_ref[...] + h, g_ref[...], b_ref[...])


def _oproj_layer(o2, x2, wo_bf, g, b, *, tm):
    n, D = x2.shape
    row = lambda: pl.BlockSpec((1, D), lambda i: (0, 0))
    return pl.pallas_call(
        _oproj_kernel,
        grid=(n // tm,),
        in_specs=[pl.BlockSpec((tm, D), lambda i: (i, 0)), pl.BlockSpec((tm, D), lambda i: (i, 0)),
                  pl.BlockSpec((D, D), lambda i: (0, 0)), row(), row()],
        out_specs=pl.BlockSpec((tm, D), lambda i: (i, 0)),
        out_shape=jax.ShapeDtypeStruct((n, D), F32),
        compiler_params=pltpu.CompilerParams(dimension_semantics=("arbitrary",)),
        name="attn_out_proj_ln",
    )(o2, x2, wo_bf, g, b)


def _search_threshold(count_ge, count_tie, rmax, rmin, ncols, ksel):
    inf = jnp.inf
    shape = rmax.shape

    def status(flo, fhi, clo):
        mid = 0.5 * flo + 0.5 * fhi
        lo_inf = flo == -inf
        hi_inf = fhi == inf
        p = mid
        p = jnp.where((flo == 0.0) & (fhi > F32_TINY), F32_TINY, p)
        p = jnp.where((flo < 0.0) & (fhi > 0.0), 0.0, p)
        p = jnp.where(lo_inf, jnp.where(fhi > rmin, rmin, -F32_MAX), p)
        p = jnp.where(hi_inf, rmax, p)
        adjacent = ~lo_inf & ~hi_inf & ((mid <= flo) | (mid >= fhi))
        done = (clo == ksel) | adjacent | (hi_inf & (flo >= rmax)) | (lo_inf & (fhi <= -F32_MAX))
        return jnp.where(done, 1.0, 0.0), p

    def search_body(st):
        it, flo, fhi, clo, chi, donef, p, _ = st
        cnt = count_ge(p)
        live = donef < 0.5
        up_lo = live & (cnt >= ksel)
        up_hi = live & (cnt < ksel)
        flo, clo = jnp.where(up_lo, p, flo), jnp.where(up_lo, cnt, clo)
        fhi, chi = jnp.where(up_hi, p, fhi), jnp.where(up_hi, cnt, chi)
        donef, p = status(flo, fhi, clo)
        return it + 1, flo, fhi, clo, chi, donef, p, (jnp.min(donef) > 0.5).astype(jnp.int32)

    flo0 = jnp.full(shape, -inf, F32)
    fhi0 = jnp.full(shape, inf, F32)
    clo0 = jnp.zeros(shape, F32) + jnp.asarray(ncols).astype(F32)
    done0, p0 = status(flo0, fhi0, clo0)
    st = lax.while_loop(lambda st: (st[0] < SEARCH_MAX_STEPS) & (st[7] == 0), search_body,
                        (jnp.int32(0), flo0, fhi0, clo0, jnp.zeros(shape, F32), done0, p0, jnp.int32(0)))
    _, thr, _, clo, chi, _, _, _ = st

    tie_rows = clo > ksel
    need = ksel - chi
    ncols_i = jnp.asarray(ncols).astype(jnp.int32)

    def tie_phase():
        def tb(_, st):
            jlo, jhi = st
            mid = (jlo + jhi) >> 1
            ok = count_tie(thr, mid) >= need
            return jnp.where(ok, jlo, mid), jnp.where(ok, mid, jhi)
        init_j = (jnp.full(shape, -1, jnp.int32), jnp.zeros(shape, jnp.int32) + (ncols_i - 1))
        _, jhi = lax.fori_loop(0, 14, tb, init_j)
        return jnp.where(tie_rows, jhi, ncols_i)

    any_tie = jnp.max(jnp.where(tie_rows, 1.0, 0.0)) > 0.0
    jcut = lax.cond(any_tie, tie_phase, lambda: jnp.zeros(shape, jnp.int32) + ncols_i)
    return thr, jcut


SUBLANES = 8
NACC = 4


def _select_bias_t(sc_ref, n_chunks, cw, tpos, ksel):
    nq = sc_ref.shape[1]
    grp = cw // SUBLANES
    inf = jnp.inf
    srow = lax.broadcasted_iota(jnp.int32, (SUBLANES, nq), 0)

    def tile(v):
        return jnp.broadcast_to(v, (SUBLANES, nq))

    def fold(tile_fn, init):
        def body(c, accs):
            accs = list(accs)
            for g in range(grp):
                first = pl.multiple_of(c * cw + g * SUBLANES, SUBLANES)
                accs[g % NACC] = tile_fn(accs[g % NACC], sc_ref[pl.ds(first, SUBLANES), :], first)
            return tuple(accs)
        return lax.fori_loop(0, n_chunks, body, tuple(init for _ in range(NACC)))

    def finish(parts, op2, op):
        acc = parts[0]
        for part in parts[1:]:
            acc = op2(acc, part)
        return op(acc, axis=0, keepdims=True)

    def count(pred):
        parts = fold(lambda a, x, first: a + pred(x, first), jnp.zeros((SUBLANES, nq), F32))
        return finish(parts, jnp.add, jnp.sum)

    parts = fold(lambda a, x, first: (jnp.maximum(a[0], x), jnp.minimum(a[1], jnp.where(x == -inf, inf, x))),
                 (jnp.full((SUBLANES, nq), -inf, F32), jnp.full((SUBLANES, nq), inf, F32)))
    rmax = finish([p[0] for p in parts], jnp.maximum, jnp.max)
    rmin = finish([p[1] for p in parts], jnp.minimum, jnp.min)

    def count_ge(p):
        pb = tile(p)
        return count(lambda x, first: jnp.where(x >= pb, 1.0, 0.0))

    def count_tie(t, j):
        tb, jb = tile(t), tile(j)
        return count(lambda x, first: jnp.where(x == tb, jnp.where(srow + first <= jb, 1.0, 0.0), 0.0))

    thr, jcut = _search_threshold(count_ge, count_tie, rmax, rmin, n_chunks * cw, ksel)
    tb, jb = tile(thr), tile(jnp.minimum(jcut, tpos))

    krow = (lax.broadcasted_iota(jnp.int32, (grp, SUBLANES, nq), 0) * SUBLANES
            + lax.broadcasted_iota(jnp.int32, (grp, SUBLANES, nq), 1))

    def write(c, carry):
        rows = pl.ds(pl.multiple_of(c * cw, cw), cw)
        x = sc_ref[rows, :].reshape(grp, SUBLANES, nq)
        tie_sel = jnp.where(krow + c * cw <= jb[None], 0.0, NEG_BIAS)
        bias = jnp.where(x > tb[None], 0.0, jnp.where(x == tb[None], tie_sel, NEG_BIAS))
        sc_ref[rows, :] = bias.reshape(cw, nq)
        return carry
    lax.fori_loop(0, n_chunks, write, 0)


def _select_bias(sc_ref, r0, nr, n_chunks, cw, tpos, ksel):
    sub = min(nr, SEL_SUB)
    nsub = nr // sub
    ncols = n_chunks * cw
    nt = cw // LANES
    lane = lax.broadcasted_iota(jnp.int32, (sub, LANES), 1)
    inf = jnp.inf

    def fold(tile_fn, init, cols=(), store=False):
        res = []
        for r in range(nsub):
            rows = slice(r0 + r * sub, r0 + (r + 1) * sub)
            bc = [jnp.broadcast_to(c[:, r * sub:(r + 1) * sub], (LANES, sub)).T for c in cols]

            def body(c, accs, rows=rows, bc=bc):
                cs = pl.ds(pl.multiple_of(c * cw, cw), cw)
                x = sc_ref[rows, cs]
                outs = []
                for j in range(nt):
                    accs = tile_fn(accs, x[:, j * LANES:(j + 1) * LANES], c * cw + j * LANES, bc)
                    if store:
                        outs.append(accs)
                if store:
                    sc_ref[rows, cs] = jnp.concatenate(outs, axis=1) if nt > 1 else outs[0]
                    return 0
                return accs
            res.append(lax.fori_loop(0, n_chunks, body, init))
        return res

    def row_reduce(parts, k, op):
        return jnp.concatenate([op(p[k].T, axis=0, keepdims=True) for p in parts], axis=1)

    def count(pred, cols):
        parts = fold(lambda a, x, base, bc: (a[0] + pred(x, base, bc),), (jnp.zeros((sub, LANES), F32),), cols)
        return row_reduce(parts, 0, jnp.sum)

    parts = fold(lambda a, x, base, bc: (jnp.maximum(a[0], x), jnp.minimum(a[1], jnp.where(x == -inf, inf, x))),
                 (jnp.full((sub, LANES), -inf, F32), jnp.full((sub, LANES), inf, F32)))
    rmax = row_reduce(parts, 0, jnp.max)
    rmin = row_reduce(parts, 1, jnp.min)

    thr, jcut = _search_threshold(
        lambda p: count(lambda x, base, bc: jnp.where(x >= bc[0], 1.0, 0.0), (p,)),
        lambda t, j: count(lambda x, base, bc: jnp.where(x == bc[0], jnp.where(lane + base <= bc[1], 1.0, 0.0), 0.0),
                           (t, j)),
        rmax, rmin, ncols, ksel)

    def bias_tile(_, x, base, bc):
        tie_sel = jnp.where(lane + base <= bc[1], 0.0, NEG_BIAS)
        return jnp.where(x > bc[0], 0.0, jnp.where(x == bc[0], tie_sel, NEG_BIAS))
    fold(bias_tile, 0, (thr, jnp.minimum(jcut, tpos)), store=True)


SEL_ROWS = 256


def _attn_prompt_t_kernel(qmap_ref, kmap_ref, qt_ref, qit_ref, wtt_ref, ki_ref, k_ref, vt_ref, o_ref,
                          sc_ref, wb_ref, m_ref, l_ref, a_ref, acc_ref, s_ref, p_ref, *, tq, tk, ksel):
    n = pl.program_id(1)
    i = qmap_ref[n]
    kj = kmap_ref[n]
    last = ((i + 1) * tq - 1) // tk
    grp = tk // SUBLANES

    def keyred(x, op):
        return op(op(x.reshape(grp // NACC, NACC, SUBLANES, tq), axis=0), axis=0)

    def rep(v8, rows):
        return jnp.concatenate([v8] * (rows // SUBLANES), axis=0)

    @pl.when(kj == 0)
    def _():
        wts = wtt_ref[...] * (IDX_DIM ** -0.5)
        for h in range(IDX_HEADS):
            wb_ref[h] = jnp.broadcast_to(wts[h:h + 1, :], (SUBLANES, tq))
        tpos = i * tq + lax.broadcasted_iota(jnp.int32, (1, tq), 1)

        def chunk_scores(c, masked):
            rows = pl.ds(pl.multiple_of(c * tk, tk), tk)
            kic = ki_ref[rows, :].astype(BF16)
            sc = None
            for h in range(IDX_HEADS):
                d = jnp.dot(kic, qit_ref[h * IDX_DIM:(h + 1) * IDX_DIM, :], preferred_element_type=F32)
                d = jnp.maximum(d, 0.0) * rep(wb_ref[h], tk)
                sc = d if sc is None else sc + d
            if masked:
                kidx = c * tk + lax.broadcasted_iota(jnp.int32, (tk, tq), 0)
                sc = jnp.where(kidx <= tpos, sc, -jnp.inf)
            sc_ref[rows, :] = sc

        def body(c, carry):
            chunk_scores(c, False)
            return carry
        lax.fori_loop(0, last, body, 0)
        chunk_scores(last, True)

        _select_bias_t(sc_ref, last + 1, tk, tpos, ksel)
        m_ref[...] = jnp.full(m_ref.shape, NEG_BIAS, F32)
        l_ref[...] = jnp.zeros(l_ref.shape, F32)
        acc_ref[...] = jnp.zeros(acc_ref.shape, F32)

    rows = pl.ds(pl.multiple_of(kj * tk, tk), tk)
    heads = [slice(h * HEAD_DIM, (h + 1) * HEAD_DIM) for h in range(N_HEADS)]
    for h, hs in enumerate(heads):
        s_ref[h] = jnp.dot(k_ref[:, hs], qt_ref[hs, :], preferred_element_type=F32) + sc_ref[rows, :]
    for h, hs in enumerate(heads):
        m_old = m_ref[h]
        smax = jnp.max(keyred(s_ref[h], jnp.max), axis=0, keepdims=True)
        m_new = jnp.maximum(m_old, jnp.broadcast_to(smax, (SUBLANES, tq)))
        a_ref[h] = jnp.exp2(m_old - m_new)
        m_ref[h] = m_new
    for h, hs in enumerate(heads):
        p = jnp.exp2(s_ref[h] - rep(m_ref[h], tk))
        l_ref[h] = a_ref[h] * l_ref[h] + keyred(p, jnp.sum)
        p_ref[h] = p.astype(BF16)
    for h, hs in enumerate(heads):
        acc_ref[h] = rep(a_ref[h], HEAD_DIM) * acc_ref[h] + jnp.dot(vt_ref[hs, :], p_ref[h],
                                                                     preferred_element_type=F32)

    @pl.when(kj == last)
    def _():
        for h, hs in enumerate(heads):
            l = jnp.broadcast_to(jnp.sum(l_ref[h], axis=0, keepdims=True), (SUBLANES, tq))
            o_ref[:, hs] = (acc_ref[h] / rep(l, HEAD_DIM)).T.astype(BF16)


def _attn_prompt_t(B, qt, qit, wtt, ki, kb, vt, *, tq=512, tk=512):
    n, D = kb.shape
    T = n // B
    ksel = min(TOPK_MAX, T // 4)
    nq, nk = T // tq, T // tk
    qmap, kmap = [], []
    for i in range(nq):
        for kj in range(((i + 1) * tq - 1) // tk + 1):
            qmap.append(i)
            kmap.append(kj)
    qmap = jnp.asarray(np.asarray(qmap, np.int32))
    kmap = jnp.asarray(np.asarray(kmap, np.int32))
    qcol = lambda r: pl.BlockSpec((r, tq), lambda b, s, qm, km: (0, b * nq + qm[s]))
    grid_spec = pltpu.PrefetchScalarGridSpec(
        num_scalar_prefetch=2,
        grid=(B, int(qmap.shape[0])),
        in_specs=[qcol(D), qcol(D_QI), qcol(IDX_HEADS),
                  pl.BlockSpec((T, IDX_DIM), lambda b, s, qm, km: (b, 0)),
                  pl.BlockSpec((tk, D), lambda b, s, qm, km: (b * nk + km[s], 0)),
                  pl.BlockSpec((D, tk), lambda b, s, qm, km: (0, b * nk + km[s]))],
        out_specs=pl.BlockSpec((tq, D), lambda b, s, qm, km: (b * nq + qm[s], 0)),
        scratch_shapes=[pltpu.VMEM((T, tq), F32),
                        pltpu.VMEM((IDX_HEADS, SUBLANES, tq), F32),
                        pltpu.VMEM((N_HEADS, SUBLANES, tq), F32),
                        pltpu.VMEM((N_HEADS, SUBLANES, tq), F32),
                        pltpu.VMEM((N_HEADS, SUBLANES, tq), F32),
                        pltpu.VMEM((N_HEADS, HEAD_DIM, tq), F32),
                        pltpu.VMEM((N_HEADS, tk, tq), F32),
                        pltpu.VMEM((N_HEADS, tk, tq), BF16)])
    return pl.pallas_call(
        functools.partial(_attn_prompt_t_kernel, tq=tq, tk=tk, ksel=ksel),
        grid_spec=grid_spec,
        out_shape=jax.ShapeDtypeStruct((n, D), BF16),
        compiler_params=pltpu.CompilerParams(dimension_semantics=("arbitrary", "arbitrary"),
                                             vmem_limit_bytes=VMEM_LIMIT),
        name="attn_prompt_t",
    )(qmap, kmap, qt, qit, wtt, ki, kb, vt)


KI_PAGES_PER_STEP = 16
KV_PAGES_PER_STEP = 8
SEL_CHUNK = 5 * LANES


def _sample_scores_kernel(pt_ref, qi_ref, wt_ref, kin_ref, *rest, n_pages, tq):
    del pt_ref
    g1 = KI_PAGES_PER_STEP
    kidx_refs, sc_ref = rest[:g1], rest[g1]
    s = pl.program_id(1)
    wcol = jnp.broadcast_to(wt_ref[0] * (IDX_DIM ** -0.5), (IDX_HEADS * tq, PAGE_SIZE))

    def chunk_scores(kit_chunk):
        d = jnp.maximum(jnp.dot(qi_ref[0], kit_chunk.astype(BF16), preferred_element_type=F32), 0.0) * wcol
        sc = d[0:tq]
        for h in range(1, IDX_HEADS):
            sc = sc + d[h * tq:(h + 1) * tq]
        return sc

    for g in range(g1):
        page = s * g1 + g
        sc_ref[0, :, pl.ds(pl.multiple_of(page * PAGE_SIZE, PAGE_SIZE), PAGE_SIZE)] = chunk_scores(kidx_refs[g][0, 0])

    @pl.when(s == n_pages // g1 - 1)
    def _():
        qrow = lax.broadcasted_iota(jnp.int32, (tq, PAGE_SIZE), 0)
        jcol = lax.broadcasted_iota(jnp.int32, (tq, PAGE_SIZE), 1)
        sc_ref[0, :, n_pages * PAGE_SIZE:] = jnp.where(jcol <= qrow, chunk_scores(kin_ref[0]), -jnp.inf)


def _sample_scores(layer, page_table, qi, wt, ki_new, cache_kidx, *, tq):
    Bd, n_pages = page_table.shape
    g1 = KI_PAGES_PER_STEP
    ncols = (n_pages + 1) * PAGE_SIZE

    def per_b(shape):
        return pl.BlockSpec((1,) + shape, lambda b, s, pt: (b,) + tuple(0 for _ in shape))

    def kidx_spec(g):
        return pl.BlockSpec((1, 1, IDX_DIM, PAGE_SIZE), lambda b, s, pt: (layer, pt[b, s * g1 + g], 0, 0))

    grid_spec = pltpu.PrefetchScalarGridSpec(
        num_scalar_prefetch=1,
        grid=(Bd, n_pages // g1),
        in_specs=[per_b((IDX_HEADS * tq, IDX_DIM)), per_b((IDX_HEADS * tq, 1)), per_b((IDX_DIM, PAGE_SIZE))]
                 + [kidx_spec(g) for g in range(g1)],
        out_specs=per_b((tq, ncols)))
    return pl.pallas_call(
        functools.partial(_sample_scores_kernel, n_pages=n_pages, tq=tq),
        grid_spec=grid_spec,
        out_shape=jax.ShapeDtypeStruct((Bd, tq, ncols), F32),
        compiler_params=pltpu.CompilerParams(dimension_semantics=("arbitrary", "arbitrary")),
        name="sample_scores",
    )(page_table, qi, wt, ki_new, *([cache_kidx] * g1))


def _sample_select_kernel(sc_ref, o_ref, *, tq, past, ksel):
    o_ref[...] = sc_ref[...]
    nr, ncols = o_ref.shape
    tpos = past + lax.rem(lax.broadcasted_iota(jnp.int32, (1, nr), 1), tq)
    _select_bias(o_ref, 0, nr, ncols // SEL_CHUNK, SEL_CHUNK, tpos, ksel)


def _sample_select(sc2, *, tq, past, ksel):
    n, ncols = sc2.shape
    nr = min(SEL_ROWS, n)
    return pl.pallas_call(
        functools.partial(_sample_select_kernel, tq=tq, past=past, ksel=ksel),
        grid=(n // nr,),
        in_specs=[pl.BlockSpec((nr, ncols), lambda i: (i, 0))],
        out_specs=pl.BlockSpec((nr, ncols), lambda i: (i, 0)),
        out_shape=jax.ShapeDtypeStruct((n, ncols), F32),
        compiler_params=pltpu.CompilerParams(dimension_semantics=("arbitrary",), vmem_limit_bytes=VMEM_LIMIT),
        name="sample_select",
    )(sc2)


def _sample_attend_kernel(pt_ref, q_ref, bias_ref, kn_ref, vn_ref, *rest, n_pages, tq):
    del pt_ref
    g2 = KV_PAGES_PER_STEP
    kpage_refs, vpage_refs = rest[:g2], rest[g2:2 * g2]
    o_ref, s_ref, l_ref, acc_ref = rest[2 * g2:]
    nsk = n_pages // g2
    s = pl.program_id(1)
    new_cols = slice(n_pages * PAGE_SIZE, (n_pages + 1) * PAGE_SIZE)

    def head_rows(h):
        return slice(h * tq, (h + 1) * tq)

    def head_cols(h):
        return slice(h * HEAD_DIM, (h + 1) * HEAD_DIM)

    @pl.when(s < nsk)
    def _():
        for g in range(g2):
            cols = pl.ds(pl.multiple_of((s * g2 + g) * PAGE_SIZE, PAGE_SIZE), PAGE_SIZE)
            bias = bias_ref[0, :, cols]
            for h in range(N_HEADS):
                kh = kpage_refs[g][0, 0, pl.ds(h, PAGE_SIZE, stride=N_HEADS), :].astype(BF16)
                s_ref[head_rows(h), cols] = _dot_nt(q_ref[0, :, head_cols(h)], kh) + bias

    @pl.when(s == nsk - 1)
    def _():
        bias = bias_ref[0, :, new_cols]
        for h in range(N_HEADS):
            s_ref[head_rows(h), new_cols] = _dot_nt(q_ref[0, :, head_cols(h)], kn_ref[0, :, head_cols(h)]) + bias
        sall = s_ref[...]
        p = jnp.exp2(sall - jnp.max(sall, axis=1, keepdims=True))
        l_ref[...] = jnp.sum(p, axis=1, keepdims=True)
        s_ref[...] = p
        acc_ref[...] = jnp.zeros(acc_ref.shape, F32)

    @pl.when(s >= nsk)
    def _():
        for h in range(N_HEADS):
            o = jnp.zeros((tq, HEAD_DIM), F32)
            for g in range(g2):
                cols = pl.ds(pl.multiple_of(((s - nsk) * g2 + g) * PAGE_SIZE, PAGE_SIZE), PAGE_SIZE)
                vh = vpage_refs[g][0, 0, pl.ds(h, PAGE_SIZE, stride=N_HEADS), :].astype(BF16)
                o = o + jnp.dot(s_ref[head_rows(h), cols].astype(BF16), vh, preferred_element_type=F32)
            acc_ref[:, head_cols(h)] += o

    @pl.when(s == 2 * nsk - 1)
    def _():
        for h in range(N_HEADS):
            o = acc_ref[:, head_cols(h)] + jnp.dot(s_ref[head_rows(h), new_cols].astype(BF16),
                                                   vn_ref[0, :, head_cols(h)], preferred_element_type=F32)
            o_ref[0, :, head_cols(h)] = (o / l_ref[head_rows(h), :]).astype(BF16)


def _sample_attend(layer, page_table, q, bias, kb_new, vb_new, cache_k, cache_v):
    Bd, tq, D = q.shape
    n_pages = page_table.shape[1]
    ncols = (n_pages + 1) * PAGE_SIZE
    g2 = KV_PAGES_PER_STEP
    nsk = n_pages // g2

    def per_b(shape):
        return pl.BlockSpec((1,) + shape, lambda b, s, pt: (b,) + tuple(0 for _ in shape))

    def page_spec(g, first_step):
        return pl.BlockSpec((1, 1, PAGE_SIZE * N_HEADS, HEAD_DIM),
                            lambda b, s, pt: (layer, pt[b, jnp.clip(s - first_step, 0, nsk - 1) * g2 + g], 0, 0))

    grid_spec = pltpu.PrefetchScalarGridSpec(
        num_scalar_prefetch=1,
        grid=(Bd, 2 * nsk),
        in_specs=[per_b((tq, D)), per_b((tq, ncols)), per_b((PAGE_SIZE, D)), per_b((PAGE_SIZE, D))]
                 + [page_spec(g, 0) for g in range(g2)] + [page_spec(g, nsk) for g in range(g2)],
        out_specs=per_b((tq, D)),
        scratch_shapes=[pltpu.VMEM((N_HEADS * tq, ncols), F32),
                        pltpu.VMEM((N_HEADS * tq, 1), F32),
                        pltpu.VMEM((tq, D), F32)])
    return pl.pallas_call(
        functools.partial(_sample_attend_kernel, n_pages=n_pages, tq=tq),
        grid_spec=grid_spec,
        out_shape=jax.ShapeDtypeStruct((Bd, tq, D), BF16),
        compiler_params=pltpu.CompilerParams(dimension_semantics=("arbitrary", "arbitrary"),
                                             vmem_limit_bytes=VMEM_LIMIT),
        name="sample_attend",
    )(page_table, q, bias, kb_new, vb_new, *([cache_k] * g2), *([cache_v] * g2))


def _pad_rows(a, n):
    return jnp.pad(a, ((0, 0), (0, n - a.shape[1]), (0, 0)))


def kernel(x_prompt, x_sample, state_pool, cache_k, cache_v, cache_kidx, page_table, pool_w, pool_scale,
           attn_w_in, attn_kn_g, attn_kn_b, attn_w_o, mlp_w1, mlp_w2, ln_g, ln_b):
    B, T, D = x_prompt.shape
    Bd, Td, _ = x_sample.shape
    n_attn, n_phys = cache_k.shape[0], cache_k.shape[1]
    n_pages = page_table.shape[1]
    past = n_pages * PAGE_SIZE
    xp, xs = x_prompt, x_sample
    ck = cache_k.reshape(n_attn, n_phys, PAGE_SIZE * N_HEADS, HEAD_DIM)
    cv = cache_v.reshape(n_attn, n_phys, PAGE_SIZE * N_HEADS, HEAD_DIM)
    ckit = cache_kidx.transpose(0, 1, 3, 2)
    pool_p, pool_s = [], []
    kip, ksm, vsm, kism = [], [], [], []
    kv_prompt = tuple(jnp.zeros((n_attn, B * T, D), F32) for _ in range(2))
    for i in range(DEPTH):
        j = i // 2
        g0, b0 = ln_g[i, 0][None], ln_b[i, 0][None]
        g1, b1 = ln_g[i, 1][None], ln_b[i, 1][None]
        if i % 2 == 0:
            w_bf = pool_w[j].astype(BF16)
            scale = pool_scale[j][None]
            pool_p.append(xp[:, T - POOL_BUF:])
            halo_s = jnp.concatenate([jnp.zeros((Bd, 1, D), xs.dtype), state_pool[j].astype(xs.dtype)], axis=1)
            pool_s.append(jnp.concatenate([halo_s, xs], axis=1)[:, -POOL_BUF:])
            xp = _pool_layer(xp, xp, w_bf, scale, g0, b0, tq=512, start=0, first_is_zero=True)
            xs = _pool_layer(xs, halo_s, w_bf, scale, g0, b0, tq=Td, start=past, first_is_zero=False)
        else:
            w_in = attn_w_in[j]
            wqkv = w_in[:, :3 * D_MODEL].astype(BF16)
            wqi = w_in[:, 3 * D_MODEL:3 * D_MODEL + D_QI].astype(BF16)
            wkw = jnp.pad(w_in[:, 3 * D_MODEL + D_QI:], ((0, 0), (0, LANES - IDX_DIM - IDX_HEADS))).astype(BF16)
            kng, knb = attn_kn_g[j][None], attn_kn_b[j][None]
            wo = attn_w_o[j].astype(BF16)

            qt, k_all, v_all, kb, vt, qit, ki, wtt = _proj_t_layer(
                xp.reshape(B * T, D), w_in, kng, knb, tm=512, layer=j, kv_all=kv_prompt)
            kv_prompt = (k_all, v_all)
            o = _attn_prompt_t(B, qt, qit, wtt, ki, kb, vt)
            xp = _oproj_layer(o, xp.reshape(B * T, D), wo, g0, b0, tm=512).reshape(B, T, D)
            kip.append(ki.reshape(B, T, IDX_DIM))

            n = Bd * Td
            q, k, v, kb, vb, qi, ki, wt = _proj_layer(xs.reshape(n, D), wqkv, wqi, wkw, kng, knb, tm=n)
            qi_s = qi.reshape(IDX_HEADS, Bd, Td, IDX_DIM).transpose(1, 0, 2, 3).reshape(Bd, IDX_HEADS * Td, IDX_DIM)
            wt_s = wt.reshape(Bd, Td, IDX_HEADS).transpose(0, 2, 1).reshape(Bd, IDX_HEADS * Td, 1)
            kit_new = _pad_rows(ki.reshape(Bd, Td, IDX_DIM), PAGE_SIZE).transpose(0, 2, 1)
            sc = _sample_scores(j, page_table, qi_s, wt_s, kit_new, ckit, tq=Td)
            bias = _sample_select(sc.reshape(n, sc.shape[-1]), tq=Td, past=past,
                                  ksel=min(TOPK_MAX, (past + Td) // 4))
            o = _sample_attend(j, page_table, q.reshape(Bd, Td, D), bias.reshape(Bd, Td, -1),
                               _pad_rows(kb.reshape(Bd, Td, D), PAGE_SIZE),
                               _pad_rows(vb.reshape(Bd, Td, D), PAGE_SIZE), ck, cv)
            xs = _oproj_layer(o.reshape(n, D), xs.reshape(n, D), wo, g0, b0, tm=n).reshape(Bd, Td, D)
            ksm.append(k.reshape(Bd, Td, N_HEADS, HEAD_DIM))
            vsm.append(v.reshape(Bd, Td, N_HEADS, HEAD_DIM))
            kism.append(ki.reshape(Bd, Td, IDX_DIM))
        w1, w2 = mlp_w1[i].astype(BF16), mlp_w2[i].astype(BF16)
        xp = _mlp_layer(xp.reshape(B * T, D), w1, w2, g1, b1, tm=512).reshape(B, T, D)
        xs = _mlp_layer(xs.reshape(Bd * Td, D), w1, w2, g1, b1, tm=Bd * Td).reshape(Bd, Td, D)
    k_prompt, v_prompt = (a.reshape(n_attn, B, T, N_HEADS, HEAD_DIM) for a in kv_prompt)
    return (xp, xs, jnp.stack(pool_p), jnp.stack(pool_s), k_prompt, v_prompt, jnp.stack(kip),
            jnp.stack(ksm), jnp.stack(vsm), jnp.stack(kism))
```

```python
import functools
import math

import numpy as np
import jax
import jax.numpy as jnp
from jax import lax
from jax.experimental import pallas as pl
from jax.experimental.pallas import tpu as pltpu

D_MODEL = 1024
DEPTH = 4
PAST_LEN = 8192
PAGE_SIZE = 128
POOL_WINDOWS = (2, 4, 8, 16)
POOL_GROUP = D_MODEL // len(POOL_WINDOWS)
POOL_BUF = max(POOL_WINDOWS) - 1
HALO = POOL_BUF + 1
N_HEADS = 8
HEAD_DIM = D_MODEL // N_HEADS
IDX_HEADS = 8
IDX_DIM = 64
TOPK_MAX = 256
D_FF = 4 * D_MODEL
ALPHA = (2 * DEPTH) ** 0.25
LN_EPS = 1e-5
D_QI = IDX_HEADS * IDX_DIM
Q_SCALE = HEAD_DIM ** -0.5 * math.log2(math.e)

LANES = 128
NEG_BIAS = -1e30
F32_MAX = float(np.finfo(np.float32).max)
F32_TINY = float(np.finfo(np.float32).tiny)
VMEM_LIMIT = 52 * 1024 * 1024
SEARCH_MAX_STEPS = 400
SEL_SUB = 128

F32 = jnp.float32
BF16 = jnp.bfloat16


def _ln(y, g, b):
    mu = jnp.mean(y, axis=-1, keepdims=True)
    yc = y - mu
    var = jnp.mean(yc * yc, axis=-1, keepdims=True)
    return yc * lax.rsqrt(var + LN_EPS) * g + b


def _dot_nt(a, b):
    return lax.dot_general(a, b, (((1,), (1,)), ((), ())), preferred_element_type=F32)


def _pool_kernel(x_ref, halo_ref, w_ref, sc_ref, g_ref, b_ref, o_ref, *, tq, start, first_is_zero):
    i = pl.program_id(1)
    x = x_ref[0]
    halo = halo_ref[0]
    if first_is_zero:
        halo = jnp.where(i == 0, 0.0, halo)
    xa = jnp.concatenate([halo, x], axis=0)
    pos = start + i * tq + lax.broadcasted_iota(jnp.int32, (tq, 1), 0)
    outs = []
    for g, w in enumerate(POOL_WINDOWS):
        sl = slice(g * POOL_GROUP, (g + 1) * POOL_GROUP)
        s = xa[:, sl]
        sh = 1
        while sh < w:
            s = s + pltpu.roll(s, sh, axis=0)
            sh *= 2
        cnt = jnp.minimum(w, pos + 1).astype(F32)
        p = s[HALO:] / cnt - x[:, sl]
        outs.append(jnp.dot(p.astype(BF16), w_ref[g], preferred_element_type=F32))
    y = jnp.concatenate(outs, axis=1) * sc_ref[...]
    o_ref[0] = _ln(ALPHA * x + y, g_ref[...], b_ref[...])


def _pool_layer(x, halo, w_bf, scale, g, b, *, tq, start, first_is_zero):
    B, T, D = x.shape
    nq = T // tq
    if first_is_zero:
        halo_spec = pl.BlockSpec((1, HALO, D), lambda bb, i: (bb, jnp.maximum(i * (tq // HALO) - 1, 0), 0))
    else:
        halo_spec = pl.BlockSpec((1, HALO, D), lambda bb, i: (bb, 0, 0))
    row = lambda: pl.BlockSpec((1, D), lambda bb, i: (0, 0))
    return pl.pallas_call(
        functools.partial(_pool_kernel, tq=tq, start=start, first_is_zero=first_is_zero),
        grid=(B, nq),
        in_specs=[pl.BlockSpec((1, tq, D), lambda bb, i: (bb, i, 0)),
                  halo_spec,
                  pl.BlockSpec((len(POOL_WINDOWS), POOL_GROUP, POOL_GROUP), lambda bb, i: (0, 0, 0)),
                  row(), row(), row()],
        out_specs=pl.BlockSpec((1, tq, D), lambda bb, i: (bb, i, 0)),
        out_shape=jax.ShapeDtypeStruct((B, T, D), F32),
        compiler_params=pltpu.CompilerParams(dimension_semantics=("arbitrary", "arbitrary")),
        name="pool_mix_ln",
    )(x, halo, w_bf, scale, g, b)


def _mlp_kernel(x_ref, w1_ref, w2_ref, g_ref, b_ref, o_ref, *, ffc):
    x = x_ref[...]
    xb = x.astype(BF16)
    acc = jnp.zeros(x.shape, F32)
    for c in range(D_FF // ffc):
        h = jnp.dot(xb, w1_ref[:, c * ffc:(c + 1) * ffc], preferred_element_type=F32)
        h = jnp.maximum(h, 0.0)
        acc = acc + jnp.dot((h * h).astype(BF16), w2_ref[c * ffc:(c + 1) * ffc, :],
                            preferred_element_type=F32)
    o_ref[...] = _ln(ALPHA * x + acc, g_ref[...], b_ref[...])


def _mlp_layer(x2, w1_bf, w2_bf, g, b, *, tm):
    n, D = x2.shape
    row = lambda: pl.BlockSpec((1, D), lambda i: (0, 0))
    return pl.pallas_call(
        functools.partial(_mlp_kernel, ffc=1024),
        grid=(n // tm,),
        in_specs=[pl.BlockSpec((tm, D), lambda i: (i, 0)),
                  pl.BlockSpec((D, D_FF), lambda i: (0, 0), pipeline_mode=pl.Buffered(1)),
                  pl.BlockSpec((D_FF, D), lambda i: (0, 0), pipeline_mode=pl.Buffered(1)),
                  row(), row()],
        out_specs=pl.BlockSpec((tm, D), lambda i: (i, 0)),
        out_shape=jax.ShapeDtypeStruct((n, D), F32),
        compiler_params=pltpu.CompilerParams(dimension_semantics=("arbitrary",),
                                             vmem_limit_bytes=VMEM_LIMIT),
        name="mlp_ln",
    )(x2, w1_bf, w2_bf, g, b)


def _proj_kernel(x_ref, wqkv_ref, wqi_ref, wkw_ref, kng_ref, knb_ref,
                 q_ref, k_ref, v_ref, kb_ref, vb_ref, qi_ref, ki_ref, wt_ref):
    xb = x_ref[...].astype(BF16)
    hq = jnp.dot(xb, wqkv_ref[:, :D_MODEL], preferred_element_type=F32)
    q_ref[...] = (hq * Q_SCALE).astype(BF16)
    for part, (f_ref, b_ref) in enumerate(((k_ref, kb_ref), (v_ref, vb_ref)), start=1):
        h = jnp.dot(xb, wqkv_ref[:, part * D_MODEL:(part + 1) * D_MODEL], preferred_element_type=F32)
        f_ref[...] = h
        b_ref[...] = h.astype(BF16)
    hqi = jnp.dot(xb, wqi_ref[...], preferred_element_type=F32)
    for hh in range(IDX_HEADS):
        qi_ref[hh] = hqi[:, hh * IDX_DIM:(hh + 1) * IDX_DIM].astype(BF16)
    hkw = jnp.dot(xb, wkw_ref[...], preferred_element_type=F32)
    ki_ref[...] = _ln(hkw[:, :IDX_DIM], kng_ref[...], knb_ref[...])
    wt_ref[...] = hkw[:, IDX_DIM:IDX_DIM + IDX_HEADS] * (IDX_HEADS ** -0.5)


def _proj_layer(x2, wqkv_bf, wqi_bf, wkw_bf, kn_g, kn_b, *, tm):
    n, D = x2.shape
    full = lambda shp: pl.BlockSpec(shp, lambda i: tuple(0 for _ in shp))
    rows = lambda c: pl.BlockSpec((tm, c), lambda i: (i, 0))
    sds = jax.ShapeDtypeStruct
    return pl.pallas_call(
        _proj_kernel,
        grid=(n // tm,),
        in_specs=[rows(D), full((D, 3 * D_MODEL)), full((D, D_QI)), full((D, LANES)),
                  full((1, IDX_DIM)), full((1, IDX_DIM))],
        out_specs=[rows(D), rows(D), rows(D), rows(D), rows(D),
                   pl.BlockSpec((IDX_HEADS, tm, IDX_DIM), lambda i: (0, i, 0)),
                   rows(IDX_DIM), rows(IDX_HEADS)],
        out_shape=[sds((n, D), BF16), sds((n, D), F32), sds((n, D), F32), sds((n, D), BF16),
                   sds((n, D), BF16), sds((IDX_HEADS, n, IDX_DIM), BF16),
                   sds((n, IDX_DIM), F32), sds((n, IDX_HEADS), F32)],
        compiler_params=pltpu.CompilerParams(dimension_semantics=("arbitrary",),
                                             vmem_limit_bytes=VMEM_LIMIT),
        name="attn_in_proj",
    )(x2, wqkv_bf, wqi_bf, wkw_bf, kn_g, kn_b)


def _proj_t_kernel(x_ref, wqt_ref, wk_ref, wv_ref, wvt_ref, wqit_ref, wkw_ref, wkwt_ref, kng_ref, knb_ref,
                   qt_ref, k_ref, v_ref, kb_ref, vt_ref, qit_ref, ki_ref, wtt_ref):
    xb = x_ref[...].astype(BF16)
    qt_ref[...] = (_dot_nt(wqt_ref[...], xb) * Q_SCALE).astype(BF16)
    hk = jnp.dot(xb, wk_ref[...], preferred_element_type=F32)
    k_ref[...] = hk
    kb_ref[...] = hk.astype(BF16)
    v_ref[...] = jnp.dot(xb, wv_ref[...], preferred_element_type=F32)
    vt_ref[...] = _dot_nt(wvt_ref[...], xb).astype(BF16)
    qit_ref[...] = _dot_nt(wqit_ref[...], xb).astype(BF16)
    hkw = jnp.dot(xb, wkw_ref[...], preferred_element_type=F32)
    ki_ref[...] = _ln(hkw[:, :IDX_DIM], kng_ref[...], knb_ref[...])
    hkwt = _dot_nt(wkwt_ref[...], xb)
    wtt_ref[...] = hkwt[IDX_DIM:IDX_DIM + IDX_HEADS, :] * (IDX_HEADS ** -0.5)


N_PROJ_T_INPUTS = 10


def _proj_t_kernel_into(*refs):
    ins = refs[:N_PROJ_T_INPUTS]
    qt_ref, k_ref, v_ref, *outs = refs[N_PROJ_T_INPUTS + 2:]
    _proj_t_kernel(*ins, qt_ref, k_ref.at[0], v_ref.at[0], *outs)


def _proj_t_layer(x2, w_in, kn_g, kn_b, *, tm, layer, kv_all):
    n, D = x2.shape
    n_layers = kv_all[0].shape[0]
    wq, wk, wv = (w_in[:, p * D_MODEL:(p + 1) * D_MODEL] for p in range(3))
    wqi = w_in[:, 3 * D_MODEL:3 * D_MODEL + D_QI]
    wkw = jnp.pad(w_in[:, 3 * D_MODEL + D_QI:], ((0, 0), (0, LANES - IDX_DIM - IDX_HEADS)))
    bf = lambda a: a.astype(BF16)
    weights = [bf(wq.T), bf(wk), bf(wv), bf(wv.T), bf(wqi.T), bf(wkw), bf(wkw.T)]
    full = lambda shp: pl.BlockSpec(shp, lambda i: tuple(0 for _ in shp))
    rows = lambda c: pl.BlockSpec((tm, c), lambda i: (i, 0))
    cols = lambda r: pl.BlockSpec((r, tm), lambda i: (0, i))
    slab = lambda: pl.BlockSpec((1, tm, D), lambda i: (layer, i, 0))
    sds = jax.ShapeDtypeStruct
    return pl.pallas_call(
        _proj_t_kernel_into,
        grid=(n // tm,),
        in_specs=[rows(D)] + [full(w.shape) for w in weights] + [full((1, IDX_DIM)), full((1, IDX_DIM))]
                 + [pl.BlockSpec(memory_space=pl.ANY), pl.BlockSpec(memory_space=pl.ANY)],
        out_specs=[cols(D), slab(), slab(), rows(D), cols(D), cols(D_QI), rows(IDX_DIM), cols(IDX_HEADS)],
        out_shape=[sds((D, n), BF16), sds((n_layers, n, D), F32), sds((n_layers, n, D), F32), sds((n, D), BF16),
                   sds((D, n), BF16), sds((D_QI, n), BF16), sds((n, IDX_DIM), F32), sds((IDX_HEADS, n), F32)],
        input_output_aliases={N_PROJ_T_INPUTS: 1, N_PROJ_T_INPUTS + 1: 2},
        compiler_params=pltpu.CompilerParams(dimension_semantics=("arbitrary",),
                                             vmem_limit_bytes=VMEM_LIMIT),
        name="attn_in_proj_t",
    )(x2, *weights, kn_g, kn_b, *kv_all)


def _oproj_kernel(o_ref, x_ref, wo_ref, g_ref, b_ref, y_ref):
    h = jnp.dot(o_ref[...], wo_ref[...], preferred_element_type=F32)
    y_ref[...] = _ln(ALPHA * x_ref[...] + h, g_ref[...], b_ref[...])


def _oproj_layer(o2, x2, wo_bf, g, b, *, tm):
    n, D = x2.shape
    row = lambda: pl.BlockSpec((1, D), lambda i: (0, 0))
    return pl.pallas_call(
        _oproj_kernel,
        grid=(n // tm,),
        in_specs=[pl.BlockSpec((tm, D), lambda i: (i, 0)), pl.BlockSpec((tm, D), lambda i: (i, 0)),
                  pl.BlockSpec((D, D), lambda i: (0, 0)), row(), row()],
        out_specs=pl.BlockSpec((tm, D), lambda i: (i, 0)),
        out_shape=jax.ShapeDtypeStruct((n, D), F32),
        compiler_params=pltpu.CompilerParams(dimension_semantics=("arbitrary",)),
        name="attn_out_proj_ln",
    )(o2, x2, wo_bf, g, b)


def _search_threshold(count_ge, count_tie, rmax, rmin, ncols, ksel):
    inf = jnp.inf
    shape = rmax.shape

    def status(flo, fhi, clo):
        mid = 0.5 * flo + 0.5 * fhi
        lo_inf = flo == -inf
        hi_inf = fhi == inf
        p = mid
        p = jnp.where((flo == 0.0) & (fhi > F32_TINY), F32_TINY, p)
        p = jnp.where((flo < 0.0) & (fhi > 0.0), 0.0, p)
        p = jnp.where(lo_inf, jnp.where(fhi > rmin, rmin, -F32_MAX), p)
        p = jnp.where(hi_inf, rmax, p)
        adjacent = ~lo_inf & ~hi_inf & ((mid <= flo) | (mid >= fhi))
        done = (clo == ksel) | adjacent | (hi_inf & (flo >= rmax)) | (lo_inf & (fhi <= -F32_MAX))
        return jnp.where(done, 1.0, 0.0), p

    def search_body(st):
        it, flo, fhi, clo, chi, donef, p, _ = st
        cnt = count_ge(p)
        live = donef < 0.5
        up_lo = live & (cnt >= ksel)
        up_hi = live & (cnt < ksel)
        flo, clo = jnp.where(up_lo, p, flo), jnp.where(up_lo, cnt, clo)
        fhi, chi = jnp.where(up_hi, p, fhi), jnp.where(up_hi, cnt, chi)
        donef, p = status(flo, fhi, clo)
        return it + 1, flo, fhi, clo, chi, donef, p, (jnp.min(donef) > 0.5).astype(jnp.int32)

    flo0 = jnp.full(shape, -inf, F32)
    fhi0 = jnp.full(shape, inf, F32)
    clo0 = jnp.zeros(shape, F32) + jnp.asarray(ncols).astype(F32)
    done0, p0 = status(flo0, fhi0, clo0)
    st = lax.while_loop(lambda st: (st[0] < SEARCH_MAX_STEPS) & (st[7] == 0), search_body,
                        (jnp.int32(0), flo0, fhi0, clo0, jnp.zeros(shape, F32), done0, p0, jnp.int32(0)))
    _, thr, _, clo, chi, _, _, _ = st

    tie_rows = clo > ksel
    need = ksel - chi
    ncols_i = jnp.asarray(ncols).astype(jnp.int32)

    def tie_phase():
        def tb(_, st):
            jlo, jhi = st
            mid = (jlo + jhi) >> 1
            ok = count_tie(thr, mid) >= need
            return jnp.where(ok, jlo, mid), jnp.where(ok, mid, jhi)
        init_j = (jnp.full(shape, -1, jnp.int32), jnp.zeros(shape, jnp.int32) + (ncols_i - 1))
        _, jhi = lax.fori_loop(0, 14, tb, init_j)
        return jnp.where(tie_rows, jhi, ncols_i)

    any_tie = jnp.max(jnp.where(tie_rows, 1.0, 0.0)) > 0.0
    jcut = lax.cond(any_tie, tie_phase, lambda: jnp.zeros(shape, jnp.int32) + ncols_i)
    return thr, jcut


SUBLANES = 8
NACC = 4


def _select_bias_t(sc_ref, n_chunks, cw, tpos, ksel):
    nq = sc_ref.shape[1]
    grp = cw // SUBLANES
    inf = jnp.inf
    srow = lax.broadcasted_iota(jnp.int32, (SUBLANES, nq), 0)

    def tile(v):
        return jnp.broadcast_to(v, (SUBLANES, nq))

    def fold(tile_fn, init):
        def body(c, accs):
            accs = list(accs)
            for g in range(grp):
                first = pl.multiple_of(c * cw + g * SUBLANES, SUBLANES)
                accs[g % NACC] = tile_fn(accs[g % NACC], sc_ref[pl.ds(first, SUBLANES), :], first)
            return tuple(accs)
        return lax.fori_loop(0, n_chunks, body, tuple(init for _ in range(NACC)))

    def finish(parts, op2, op):
        acc = parts[0]
        for part in parts[1:]:
            acc = op2(acc, part)
        return op(acc, axis=0, keepdims=True)

    def count(pred):
        parts = fold(lambda a, x, first: a + pred(x, first), jnp.zeros((SUBLANES, nq), F32))
        return finish(parts, jnp.add, jnp.sum)

    parts = fold(lambda a, x, first: (jnp.maximum(a[0], x), jnp.minimum(a[1], jnp.where(x == -inf, inf, x))),
                 (jnp.full((SUBLANES, nq), -inf, F32), jnp.full((SUBLANES, nq), inf, F32)))
    rmax = finish([p[0] for p in parts], jnp.maximum, jnp.max)
    rmin = finish([p[1] for p in parts], jnp.minimum, jnp.min)

    def count_ge(p):
        pb = tile(p)
        return count(lambda x, first: jnp.where(x >= pb, 1.0, 0.0))

    def count_tie(t, j):
        tb, jb = tile(t), tile(j)
        return count(lambda x, first: jnp.where(x == tb, jnp.where(srow + first <= jb, 1.0, 0.0), 0.0))

    thr, jcut = _search_threshold(count_ge, count_tie, rmax, rmin, n_chunks * cw, ksel)
    tb, jb = tile(thr), tile(jnp.minimum(jcut, tpos))

    krow = (lax.broadcasted_iota(jnp.int32, (grp, SUBLANES, nq), 0) * SUBLANES
            + lax.broadcasted_iota(jnp.int32, (grp, SUBLANES, nq), 1))

    def write(c, carry):
        rows = pl.ds(pl.multiple_of(c * cw, cw), cw)
        x = sc_ref[rows, :].reshape(grp, SUBLANES, nq)
        tie_sel = jnp.where(krow + c * cw <= jb[None], 0.0, NEG_BIAS)
        bias = jnp.where(x > tb[None], 0.0, jnp.where(x == tb[None], tie_sel, NEG_BIAS))
        sc_ref[rows, :] = bias.reshape(cw, nq)
        return carry
    lax.fori_loop(0, n_chunks, write, 0)


def _select_bias(sc_ref, r0, nr, n_chunks, cw, tpos, ksel):
    sub = min(nr, SEL_SUB)
    nsub = nr // sub
    ncols = n_chunks * cw
    nt = cw // LANES
    lane = lax.broadcasted_iota(jnp.int32, (sub, LANES), 1)
    inf = jnp.inf

    def fold(tile_fn, init, cols=(), store=False):
        res = []
        for r in range(nsub):
            rows = slice(r0 + r * sub, r0 + (r + 1) * sub)
            bc = [jnp.broadcast_to(c[:, r * sub:(r + 1) * sub], (LANES, sub)).T for c in cols]

            def body(c, accs, rows=rows, bc=bc):
                cs = pl.ds(pl.multiple_of(c * cw, cw), cw)
                x = sc_ref[rows, cs]
                outs = []
                for j in range(nt):
                    accs = tile_fn(accs, x[:, j * LANES:(j + 1) * LANES], c * cw + j * LANES, bc)
                    if store:
                        outs.append(accs)
                if store:
                    sc_ref[rows, cs] = jnp.concatenate(outs, axis=1) if nt > 1 else outs[0]
                    return 0
                return accs
            res.append(lax.fori_loop(0, n_chunks, body, init))
        return res

    def row_reduce(parts, k, op):
        return jnp.concatenate([op(p[k].T, axis=0, keepdims=True) for p in parts], axis=1)

    def count(pred, cols):
        parts = fold(lambda a, x, base, bc: (a[0] + pred(x, base, bc),), (jnp.zeros((sub, LANES), F32),), cols)
        return row_reduce(parts, 0, jnp.sum)

    parts = fold(lambda a, x, base, bc: (jnp.maximum(a[0], x), jnp.minimum(a[1], jnp.where(x == -inf, inf, x))),
                 (jnp.full((sub, LANES), -inf, F32), jnp.full((sub, LANES), inf, F32)))
    rmax = row_reduce(parts, 0, jnp.max)
    rmin = row_reduce(parts, 1, jnp.min)

    thr, jcut = _search_threshold(
        lambda p: count(lambda x, base, bc: jnp.where(x >= bc[0], 1.0, 0.0), (p,)),
        lambda t, j: count(lambda x, base, bc: jnp.where(x == bc[0], jnp.where(lane + base <= bc[1], 1.0, 0.0), 0.0),
                           (t, j)),
        rmax, rmin, ncols, ksel)

    def bias_tile(_, x, base, bc):
        tie_sel = jnp.where(lane + base <= bc[1], 0.0, NEG_BIAS)
        return jnp.where(x > bc[0], 0.0, jnp.where(x == bc[0], tie_sel, NEG_BIAS))
    fold(bias_tile, 0, (thr, jnp.minimum(jcut, tpos)), store=True)


SEL_ROWS = 256


def _attn_prompt_t_kernel(qmap_ref, kmap_ref, qt_ref, qit_ref, wtt_ref, ki_ref, k_ref, vt_ref,
                          x_ref, wo_ref, g_ref, b_ref, y_ref,
                          sc_ref, wb_ref, m_ref, l_ref, a_ref, acc_ref, s_ref, p_ref, o_ref, *, tq, tk, ksel):
    n = pl.program_id(1)
    i = qmap_ref[n]
    kj = kmap_ref[n]
    last = ((i + 1) * tq - 1) // tk
    grp = tk // SUBLANES

    def keyred(x, op):
        return op(op(x.reshape(grp // NACC, NACC, SUBLANES, tq), axis=0), axis=0)

    def rep(v8, rows):
        return jnp.concatenate([v8] * (rows // SUBLANES), axis=0)

    @pl.when(kj == 0)
    def _():
        wts = wtt_ref[...] * (IDX_DIM ** -0.5)
        for h in range(IDX_HEADS):
            wb_ref[h] = jnp.broadcast_to(wts[h:h + 1, :], (SUBLANES, tq))
        tpos = i * tq + lax.broadcasted_iota(jnp.int32, (1, tq), 1)

        def chunk_scores(c, masked):
            rows = pl.ds(pl.multiple_of(c * tk, tk), tk)
            kic = ki_ref[rows, :].astype(BF16)
            sc = None
            for h in range(IDX_HEADS):
                d = jnp.dot(kic, qit_ref[h * IDX_DIM:(h + 1) * IDX_DIM, :], preferred_element_type=F32)
                d = jnp.maximum(d, 0.0) * rep(wb_ref[h], tk)
                sc = d if sc is None else sc + d
            if masked:
                kidx = c * tk + lax.broadcasted_iota(jnp.int32, (tk, tq), 0)
                sc = jnp.where(kidx <= tpos, sc, -jnp.inf)
            sc_ref[rows, :] = sc

        def body(c, carry):
            chunk_scores(c, False)
            return carry
        lax.fori_loop(0, last, body, 0)
        chunk_scores(last, True)

        _select_bias_t(sc_ref, last + 1, tk, tpos, ksel)
        m_ref[...] = jnp.full(m_ref.shape, NEG_BIAS, F32)
        l_ref[...] = jnp.zeros(l_ref.shape, F32)
        acc_ref[...] = jnp.zeros(acc_ref.shape, F32)

    rows = pl.ds(pl.multiple_of(kj * tk, tk), tk)
    heads = [slice(h * HEAD_DIM, (h + 1) * HEAD_DIM) for h in range(N_HEADS)]
    for h, hs in enumerate(heads):
        s_ref[h] = jnp.dot(k_ref[:, hs], qt_ref[hs, :], preferred_element_type=F32) + sc_ref[rows, :]
    for h, hs in enumerate(heads):
        m_old = m_ref[h]
        smax = jnp.max(keyred(s_ref[h], jnp.max), axis=0, keepdims=True)
        m_new = jnp.maximum(m_old, jnp.broadcast_to(smax, (SUBLANES, tq)))
        a_ref[h] = jnp.exp2(m_old - m_new)
        m_ref[h] = m_new
    for h, hs in enumerate(heads):
        p = jnp.exp2(s_ref[h] - rep(m_ref[h], tk))
        l_ref[h] = a_ref[h] * l_ref[h] + keyred(p, jnp.sum)
        p_ref[h] = p.astype(BF16)
    for h, hs in enumerate(heads):
        acc_ref[h] = rep(a_ref[h], HEAD_DIM) * acc_ref[h] + jnp.dot(vt_ref[hs, :], p_ref[h],
                                                                     preferred_element_type=F32)

    @pl.when(kj == last)
    def _():
        for h, hs in enumerate(heads):
            l = jnp.broadcast_to(jnp.sum(l_ref[h], axis=0, keepdims=True), (SUBLANES, tq))
            o_ref[:, hs] = (acc_ref[h] / rep(l, HEAD_DIM)).T.astype(BF16)
        hproj = jnp.dot(o_ref[...], wo_ref[...], preferred_element_type=F32)
        y_ref[...] = _ln(ALPHA * x_ref[...] + hproj, g_ref[...], b_ref[...])


def _attn_prompt_t(B, qt, qit, wtt, ki, kb, vt, x2, wo_bf, g, b, *, tq=512, tk=512):
    n, D = kb.shape
    T = n // B
    ksel = min(TOPK_MAX, T // 4)
    nq, nk = T // tq, T // tk
    qmap, kmap = [], []
    for i in range(nq):
        for kj in range(((i + 1) * tq - 1) // tk + 1):
            qmap.append(i)
            kmap.append(kj)
    qmap = jnp.asarray(np.asarray(qmap, np.int32))
    kmap = jnp.asarray(np.asarray(kmap, np.int32))
    qcol = lambda r: pl.BlockSpec((r, tq), lambda b, s, qm, km: (0, b * nq + qm[s]))
    grid_spec = pltpu.PrefetchScalarGridSpec(
        num_scalar_prefetch=2,
        grid=(B, int(qmap.shape[0])),
        in_specs=[qcol(D), qcol(D_QI), qcol(IDX_HEADS),
                  pl.BlockSpec((T, IDX_DIM), lambda b, s, qm, km: (b, 0), pipeline_mode=pl.Buffered(1)),
                  pl.BlockSpec((tk, D), lambda b, s, qm, km: (b * nk + km[s], 0)),
                  pl.BlockSpec((D, tk), lambda b, s, qm, km: (0, b * nk + km[s])),
                  pl.BlockSpec((tq, D), lambda b, s, qm, km: (b * nq + qm[s], 0)),
                  pl.BlockSpec((D, D), lambda b, s, qm, km: (0, 0), pipeline_mode=pl.Buffered(1)),
                  pl.BlockSpec((1, D), lambda b, s, qm, km: (0, 0)),
                  pl.BlockSpec((1, D), lambda b, s, qm, km: (0, 0))],
        out_specs=pl.BlockSpec((tq, D), lambda b, s, qm, km: (b * nq + qm[s], 0)),
        scratch_shapes=[pltpu.VMEM((T, tq), F32),
                        pltpu.VMEM((IDX_HEADS, SUBLANES, tq), F32),
                        pltpu.VMEM((N_HEADS, SUBLANES, tq), F32),
                        pltpu.VMEM((N_HEADS, SUBLANES, tq), F32),
                        pltpu.VMEM((N_HEADS, SUBLANES, tq), F32),
                        pltpu.VMEM((N_HEADS, HEAD_DIM, tq), F32),
                        pltpu.VMEM((N_HEADS, tk, tq), F32),
                        pltpu.VMEM((N_HEADS, tk, tq), BF16),
                        pltpu.VMEM((tq, D), BF16)])
    return pl.pallas_call(
        functools.partial(_attn_prompt_t_kernel, tq=tq, tk=tk, ksel=ksel),
        grid_spec=grid_spec,
        out_shape=jax.ShapeDtypeStruct((n, D), F32),
        compiler_params=pltpu.CompilerParams(dimension_semantics=("arbitrary", "arbitrary"),
                                             vmem_limit_bytes=VMEM_LIMIT),
        name="attn_prompt_t",
    )(qmap, kmap, qt, qit, wtt, ki, kb, vt, x2, wo_bf, g, b)


KI_PAGES_PER_STEP = 16
KV_PAGES_PER_STEP = 8
SEL_CHUNK = 5 * LANES


def _sample_scores_kernel(pt_ref, qi_ref, wt_ref, kin_ref, *rest, n_pages, tq):
    del pt_ref
    g1 = KI_PAGES_PER_STEP
    kidx_refs, sc_ref = rest[:g1], rest[g1]
    s = pl.program_id(1)
    wcol = jnp.broadcast_to(wt_ref[0] * (IDX_DIM ** -0.5), (IDX_HEADS * tq, PAGE_SIZE))

    def chunk_scores(kit_chunk):
        d = jnp.maximum(jnp.dot(qi_ref[0], kit_chunk.astype(BF16), preferred_element_type=F32), 0.0) * wcol
        sc = d[0:tq]
        for h in range(1, IDX_HEADS):
            sc = sc + d[h * tq:(h + 1) * tq]
        return sc

    for g in range(g1):
        page = s * g1 + g
        sc_ref[0, :, pl.ds(pl.multiple_of(page * PAGE_SIZE, PAGE_SIZE), PAGE_SIZE)] = chunk_scores(kidx_refs[g][0, 0])

    @pl.when(s == n_pages // g1 - 1)
    def _():
        qrow = lax.broadcasted_iota(jnp.int32, (tq, PAGE_SIZE), 0)
        jcol = lax.broadcasted_iota(jnp.int32, (tq, PAGE_SIZE), 1)
        sc_ref[0, :, n_pages * PAGE_SIZE:] = jnp.where(jcol <= qrow, chunk_scores(kin_ref[0]), -jnp.inf)


def _sample_scores(layer, page_table, qi, wt, ki_new, cache_kidx, *, tq):
    Bd, n_pages = page_table.shape
    g1 = KI_PAGES_PER_STEP
    ncols = (n_pages + 1) * PAGE_SIZE

    def per_b(shape):
        return pl.BlockSpec((1,) + shape, lambda b, s, pt: (b,) + tuple(0 for _ in shape))

    def kidx_spec(g):
        return pl.BlockSpec((1, 1, IDX_DIM, PAGE_SIZE), lambda b, s, pt: (layer, pt[b, s * g1 + g], 0, 0))

    grid_spec = pltpu.PrefetchScalarGridSpec(
        num_scalar_prefetch=1,
        grid=(Bd, n_pages // g1),
        in_specs=[per_b((IDX_HEADS * tq, IDX_DIM)), per_b((IDX_HEADS * tq, 1)), per_b((IDX_DIM, PAGE_SIZE))]
                 + [kidx_spec(g) for g in range(g1)],
        out_specs=per_b((tq, ncols)))
    return pl.pallas_call(
        functools.partial(_sample_scores_kernel, n_pages=n_pages, tq=tq),
        grid_spec=grid_spec,
        out_shape=jax.ShapeDtypeStruct((Bd, tq, ncols), F32),
        compiler_params=pltpu.CompilerParams(dimension_semantics=("arbitrary", "arbitrary")),
        name="sample_scores",
    )(page_table, qi, wt, ki_new, *([cache_kidx] * g1))


def _sample_select_kernel(sc_ref, o_ref, *, tq, past, ksel):
    o_ref[...] = sc_ref[...]
    nr, ncols = o_ref.shape
    tpos = past + lax.rem(lax.broadcasted_iota(jnp.int32, (1, nr), 1), tq)
    _select_bias(o_ref, 0, nr, ncols // SEL_CHUNK, SEL_CHUNK, tpos, ksel)


def _sample_select(sc2, *, tq, past, ksel):
    n, ncols = sc2.shape
    nr = min(SEL_ROWS, n)
    return pl.pallas_call(
        functools.partial(_sample_select_kernel, tq=tq, past=past, ksel=ksel),
        grid=(n // nr,),
        in_specs=[pl.BlockSpec((nr, ncols), lambda i: (i, 0))],
        out_specs=pl.BlockSpec((nr, ncols), lambda i: (i, 0)),
        out_shape=jax.ShapeDtypeStruct((n, ncols), F32),
        compiler_params=pltpu.CompilerParams(dimension_semantics=("arbitrary",), vmem_limit_bytes=VMEM_LIMIT),
        name="sample_select",
    )(sc2)


def _sample_attend_kernel(pt_ref, q_ref, bias_ref, kn_ref, vn_ref, *rest, n_pages, tq):
    del pt_ref
    g2 = KV_PAGES_PER_STEP
    kpage_refs, vpage_refs = rest[:g2], rest[g2:2 * g2]
    o_ref, s_ref, l_ref, acc_ref = rest[2 * g2:]
    nsk = n_pages // g2
    s = pl.program_id(1)
    new_cols = slice(n_pages * PAGE_SIZE, (n_pages + 1) * PAGE_SIZE)

    def head_rows(h):
        return slice(h * tq, (h + 1) * tq)

    def head_cols(h):
        return slice(h * HEAD_DIM, (h + 1) * HEAD_DIM)

    @pl.when(s < nsk)
    def _():
        for g in range(g2):
            cols = pl.ds(pl.multiple_of((s * g2 + g) * PAGE_SIZE, PAGE_SIZE), PAGE_SIZE)
            bias = bias_ref[0, :, cols]
            for h in range(N_HEADS):
                kh = kpage_refs[g][0, 0, pl.ds(h, PAGE_SIZE, stride=N_HEADS), :].astype(BF16)
                s_ref[head_rows(h), cols] = _dot_nt(q_ref[0, :, head_cols(h)], kh) + bias

    @pl.when(s == nsk - 1)
    def _():
        bias = bias_ref[0, :, new_cols]
        for h in range(N_HEADS):
            s_ref[head_rows(h), new_cols] = _dot_nt(q_ref[0, :, head_cols(h)], kn_ref[0, :, head_cols(h)]) + bias
        sall = s_ref[...]
        p = jnp.exp2(sall - jnp.max(sall, axis=1, keepdims=True))
        l_ref[...] = jnp.sum(p, axis=1, keepdims=True)
        s_ref[...] = p
        acc_ref[...] = jnp.zeros(acc_ref.shape, F32)

    @pl.when(s >= nsk)
    def _():
        for h in range(N_HEADS):
            o = jnp.zeros((tq, HEAD_DIM), F32)
            for g in range(g2):
                cols = pl.ds(pl.multiple_of(((s - nsk) * g2 + g) * PAGE_SIZE, PAGE_SIZE), PAGE_SIZE)
                vh = vpage_refs[g][0, 0, pl.ds(h, PAGE_SIZE, stride=N_HEADS), :].astype(BF16)
                o = o + jnp.dot(s_ref[head_rows(h), cols].astype(BF16), vh, preferred_element_type=F32)
            acc_ref[:, head_cols(h)] += o

    @pl.when(s == 2 * nsk - 1)
    def _():
        for h in range(N_HEADS):
            o = acc_ref[:, head_cols(h)] + jnp.dot(s_ref[head_rows(h), new_cols].astype(BF16),
                                                   vn_ref[0, :, head_cols(h)], preferred_element_type=F32)
            o_ref[0, :, head_cols(h)] = (o / l_ref[head_rows(h), :]).astype(BF16)


def _sample_attend(layer, page_table, q, bias, kb_new, vb_new, cache_k, cache_v):
    Bd, tq, D = q.shape
    n_pages = page_table.shape[1]
    ncols = (n_pages + 1) * PAGE_SIZE
    g2 = KV_PAGES_PER_STEP
    nsk = n_pages // g2

    def per_b(shape):
        return pl.BlockSpec((1,) + shape, lambda b, s, pt: (b,) + tuple(0 for _ in shape))

    def page_spec(g, first_step):
        return pl.BlockSpec((1, 1, PAGE_SIZE * N_HEADS, HEAD_DIM),
                            lambda b, s, pt: (layer, pt[b, jnp.clip(s - first_step, 0, nsk - 1) * g2 + g], 0, 0))

    grid_spec = pltpu.PrefetchScalarGridSpec(
        num_scalar_prefetch=1,
        grid=(Bd, 2 * nsk),
        in_specs=[per_b((tq, D)), per_b((tq, ncols)), per_b((PAGE_SIZE, D)), per_b((PAGE_SIZE, D))]
                 + [page_spec(g, 0) for g in range(g2)] + [page_spec(g, nsk) for g in range(g2)],
        out_specs=per_b((tq, D)),
        scratch_shapes=[pltpu.VMEM((N_HEADS * tq, ncols), F32),
                        pltpu.VMEM((N_HEADS * tq, 1), F32),
                        pltpu.VMEM((tq, D), F32)])
    return pl.pallas_call(
        functools.partial(_sample_attend_kernel, n_pages=n_pages, tq=tq),
        grid_spec=grid_spec,
        out_shape=jax.ShapeDtypeStruct((Bd, tq, D), BF16),
        compiler_params=pltpu.CompilerParams(dimension_semantics=("arbitrary", "arbitrary"),
                                             vmem_limit_bytes=VMEM_LIMIT),
        name="sample_attend",
    )(page_table, q, bias, kb_new, vb_new, *([cache_k] * g2), *([cache_v] * g2))


def _pad_rows(a, n):
    return jnp.pad(a, ((0, 0), (0, n - a.shape[1]), (0, 0)))


def kernel(x_prompt, x_sample, state_pool, cache_k, cache_v, cache_kidx, page_table, pool_w, pool_scale,
           attn_w_in, attn_kn_g, attn_kn_b, attn_w_o, mlp_w1, mlp_w2, ln_g, ln_b):
    B, T, D = x_prompt.shape
    Bd, Td, _ = x_sample.shape
    n_attn, n_phys = cache_k.shape[0], cache_k.shape[1]
    n_pages = page_table.shape[1]
    past = n_pages * PAGE_SIZE
    xp, xs = x_prompt, x_sample
    ck = cache_k.reshape(n_attn, n_phys, PAGE_SIZE * N_HEADS, HEAD_DIM)
    cv = cache_v.reshape(n_attn, n_phys, PAGE_SIZE * N_HEADS, HEAD_DIM)
    ckit = cache_kidx.transpose(0, 1, 3, 2)
    pool_p, pool_s = [], []
    kip, ksm, vsm, kism = [], [], [], []
    kv_prompt = tuple(jnp.zeros((n_attn, B * T, D), F32) for _ in range(2))
    for i in range(DEPTH):
        j = i // 2
        g0, b0 = ln_g[i, 0][None], ln_b[i, 0][None]
        g1, b1 = ln_g[i, 1][None], ln_b[i, 1][None]
        if i % 2 == 0:
            w_bf = pool_w[j].astype(BF16)
            scale = pool_scale[j][None]
            pool_p.append(xp[:, T - POOL_BUF:])
            halo_s = jnp.concatenate([jnp.zeros((Bd, 1, D), xs.dtype), state_pool[j].astype(xs.dtype)], axis=1)
            pool_s.append(jnp.concatenate([halo_s, xs], axis=1)[:, -POOL_BUF:])
            xp = _pool_layer(xp, xp, w_bf, scale, g0, b0, tq=512, start=0, first_is_zero=True)
            xs = _pool_layer(xs, halo_s, w_bf, scale, g0, b0, tq=Td, start=past, first_is_zero=False)
        else:
            w_in = attn_w_in[j]
            wqkv = w_in[:, :3 * D_MODEL].astype(BF16)
            wqi = w_in[:, 3 * D_MODEL:3 * D_MODEL + D_QI].astype(BF16)
            wkw = jnp.pad(w_in[:, 3 * D_MODEL + D_QI:], ((0, 0), (0, LANES - IDX_DIM - IDX_HEADS))).astype(BF16)
            kng, knb = attn_kn_g[j][None], attn_kn_b[j][None]
            wo = attn_w_o[j].astype(BF16)

            qt, k_all, v_all, kb, vt, qit, ki, wtt = _proj_t_layer(
                xp.reshape(B * T, D), w_in, kng, knb, tm=512, layer=j, kv_all=kv_prompt)
            kv_prompt = (k_all, v_all)
            xp = _attn_prompt_t(B, qt, qit, wtt, ki, kb, vt, xp.reshape(B * T, D), wo, g0, b0).reshape(B, T, D)
            kip.append(ki.reshape(B, T, IDX_DIM))

            n = Bd * Td
            q, k, v, kb, vb, qi, ki, wt = _proj_layer(xs.reshape(n, D), wqkv, wqi, wkw, kng, knb, tm=n)
            qi_s = qi.reshape(IDX_HEADS, Bd, Td, IDX_DIM).transpose(1, 0, 2, 3).reshape(Bd, IDX_HEADS * Td, IDX_DIM)
            wt_s = wt.reshape(Bd, Td, IDX_HEADS).transpose(0, 2, 1).reshape(Bd, IDX_HEADS * Td, 1)
            kit_new = _pad_rows(ki.reshape(Bd, Td, IDX_DIM), PAGE_SIZE).transpose(0, 2, 1)
            sc = _sample_scores(j, page_table, qi_s, wt_s, kit_new, ckit, tq=Td)
            bias = _sample_select(sc.reshape(n, sc.shape[-1]), tq=Td, past=past,
                                  ksel=min(TOPK_MAX, (past + Td) // 4))
            o = _sample_attend(j, page_table, q.reshape(Bd, Td, D), bias.reshape(Bd, Td, -1),
                               _pad_rows(kb.reshape(Bd, Td, D), PAGE_SIZE),
                               _pad_rows(vb.reshape(Bd, Td, D), PAGE_SIZE), ck, cv)
            xs = _oproj_layer(o.reshape(n, D), xs.reshape(n, D), wo, g0, b0, tm=n).reshape(Bd, Td, D)
            ksm.append(k.reshape(Bd, Td, N_HEADS, HEAD_DIM))
            vsm.append(v.reshape(Bd, Td, N_HEADS, HEAD_DIM))
            kism.append(ki.reshape(Bd, Td, IDX_DIM))
        w1, w2 = mlp_w1[i].astype(BF16), mlp_w2[i].astype(BF16)
        xp = _mlp_layer(xp.reshape(B * T, D), w1, w2, g1, b1, tm=512).reshape(B, T, D)
        xs = _mlp_layer(xs.reshape(Bd * Td, D), w1, w2, g1, b1, tm=Bd * Td).reshape(Bd, Td, D)
    k_prompt, v_prompt = (a.reshape(n_attn, B, T, N_HEADS, HEAD_DIM) for a in kv_prompt)
    return (xp, xs, jnp.stack(pool_p), jnp.stack(pool_s), k_prompt, v_prompt, jnp.stack(kip),
            jnp.stack(ksm), jnp.stack(vsm), jnp.stack(kism))
```
